```python
import jax, jax.numpy as jnp
from jax import lax
import numpy as np

D_MODEL = 1024
BATCH = 2
SEQ = 8192
DEPTH = 2
DEC_BATCH = 128
DEC_SEQ = 8
PAST_LEN = 2048
PAGE_SIZE = 128

N_HEADS_A = 8
HEAD_DIM = 64
WIDTH_A = N_HEADS_A * HEAD_DIM
N_GROUPS_B = 8
GROUP_DIM = 64
WIDTH_B = N_GROUPS_B * GROUP_DIM
CHUNK = 128
Q_BLOCK = 128
EPS = 1e-6
IN_COLS = 4 * WIDTH_A + N_HEADS_A + 3 * WIDTH_B + 2 * D_MODEL

kernel_name = "fox_gmlp_gated_hybrid_step"


def rms_norm(x, g):
    xf = x.astype(jnp.float32)
    r = lax.rsqrt(jnp.mean(xf * xf, axis=-1, keepdims=True) + EPS)
    return (xf * r).astype(x.dtype) * g


def in_proj(h, w_in, b_forget, q_norm_g, k_norm_g, v_norm_g):
    z = h @ w_in
    sizes = (WIDTH_A,) * 4 + (N_HEADS_A,) + (WIDTH_B,) * 3 + (D_MODEL,) * 2
    idx = [int(i) for i in np.cumsum(sizes)[:-1]]
    q, k, v, gate_a, f_logit, u, vb, gate_b, g_a, g_b = jnp.split(z, idx, axis=-1)
    hs = h.shape[:-1] + (N_HEADS_A, HEAD_DIM)
    q = rms_norm(q.reshape(hs), q_norm_g)
    k = rms_norm(k.reshape(hs), k_norm_g)
    v = v.reshape(hs)
    logf = jax.nn.log_sigmoid((f_logit + b_forget).astype(jnp.float32))
    u = jax.nn.gelu(u)
    vb = rms_norm(jax.nn.gelu(vb), v_norm_g)
    return q, k, v, gate_a, logf, u, vb, gate_b, g_a, g_b


def fox_prompt(q, k, v, logf):
    B, S = q.shape[0], q.shape[1]
    c = jnp.cumsum(logf, axis=1).transpose(0, 2, 1)
    scale = HEAD_DIM ** -0.5
    kpos = jnp.arange(S)

    def block(i):
        start = i * Q_BLOCK
        qb = lax.dynamic_slice_in_dim(q, start, Q_BLOCK, axis=1)
        cb = lax.dynamic_slice_in_dim(c, start, Q_BLOCK, axis=2)
        s = jnp.einsum('bqhd,bkhd->bhqk', qb, k).astype(jnp.float32) * scale
        s = s + (cb[..., :, None] - c[..., None, :])
        qpos = start + jnp.arange(Q_BLOCK)
        s = jnp.where(qpos[:, None] >= kpos[None, :], s, -jnp.inf)
        p = jax.nn.softmax(s, axis=-1).astype(v.dtype)
        return jnp.einsum('bhqk,bkhd->bqhd', p, v)

    out = lax.map(block, jnp.arange(S // Q_BLOCK))
    return out.transpose(1, 0, 2, 3, 4).reshape(B, S, N_HEADS_A, HEAD_DIM)


def fox_sample(q, k, v, logf, ck, cv, cl, page_table):
    DB, T = q.shape[0], q.shape[1]
    kp = ck[page_table].reshape(DB, -1, N_HEADS_A, HEAD_DIM)
    vp = cv[page_table].reshape(DB, -1, N_HEADS_A, HEAD_DIM)
    lp = cl[page_table].reshape(DB, -1, N_HEADS_A).astype(jnp.float32)
    P = kp.shape[1]
    c_past = jnp.cumsum(lp, axis=1)
    c_new = c_past[:, -1:, :] + jnp.cumsum(logf, axis=1)
    c_past = c_past.transpose(0, 2, 1)
    c_new = c_new.transpose(0, 2, 1)
    scale = HEAD_DIM ** -0.5
    s_past = jnp.einsum('bqhd,bkhd->bhqk', q, kp).astype(jnp.float32) * scale
    s_past = s_past + (c_new[..., :, None] - c_past[..., None, :])
    s_new = jnp.einsum('bqhd,bkhd->bhqk', q, k).astype(jnp.float32) * scale
    s_new = s_new + (c_new[..., :, None] - c_new[..., None, :])
    tpos = jnp.arange(T)
    s_new = jnp.where(tpos[:, None] >= tpos[None, :], s_new, -jnp.inf)
    p = jax.nn.softmax(jnp.concatenate([s_past, s_new], axis=-1), axis=-1).astype(v.dtype)
    return (jnp.einsum('bhqk,bkhd->bqhd', p[..., :P], vp)
            + jnp.einsum('bhqk,bkhd->bqhd', p[..., P:], v))


def sgu_prompt(vb, w_s, b_s):
    B, S = vb.shape[0], vb.shape[1]
    vc = vb.reshape(B, S // CHUNK, CHUNK, N_GROUPS_B, GROUP_DIM)
    w = w_s * jnp.tril(jnp.ones((CHUNK, CHUNK), w_s.dtype))
    s = jnp.einsum('gts,bnsgd->bntgd', w, vc) + b_s.T[None, None, :, :, None]
    return s.reshape(B, S, WIDTH_B)


def sgu_sample(vb, w_s, b_s):
    DB, T = vb.shape[0], vb.shape[1]
    vc = vb.reshape(DB, T, N_GROUPS_B, GROUP_DIM)
    w = w_s[:, :T, :T] * jnp.tril(jnp.ones((T, T), w_s.dtype))
    s = jnp.einsum('gts,bsgd->btgd', w, vc) + b_s[:, :T].T[None, :, :, None]
    return s.reshape(DB, T, WIDTH_B)


def merge_out(attn, s_sgu, u, gate_a, gate_b, g_a, g_b, w_pa, w_pb, w_o):
    a = attn.reshape(attn.shape[:-2] + (WIDTH_A,)) * jax.nn.silu(gate_a)
    b = u * s_sgu * jax.nn.silu(gate_b)
    m = jax.nn.sigmoid(g_a) * (a @ w_pa) + jax.nn.sigmoid(g_b) * (b @ w_pb)
    return m @ w_o


def setup_inputs(seed: int = 0) -> dict:
    key = jax.random.key(seed)
    ks = jax.random.split(key, 20)
    n_pages = PAST_LEN // PAGE_SIZE
    n_used = DEC_BATCH * n_pages
    n_pool = n_used + n_used // 4
    f32 = jnp.float32
    nrm = lambda k, shape, s: jax.random.normal(k, shape, f32) * s
    page_table = jax.random.permutation(ks[0], n_pool)[:n_used].reshape(DEC_BATCH, n_pages).astype(jnp.int32)
    return {
        "x_prompt": nrm(ks[1], (BATCH, SEQ, D_MODEL), 1.0),
        "x_sample": nrm(ks[2], (DEC_BATCH, DEC_SEQ, D_MODEL), 1.0),
        "cache_k": nrm(ks[3], (DEPTH, n_pool, PAGE_SIZE, N_HEADS_A, HEAD_DIM), 1.0),
        "cache_v": nrm(ks[4], (DEPTH, n_pool, PAGE_SIZE, N_HEADS_A, HEAD_DIM), 1.0),
        "cache_logf": jax.nn.log_sigmoid(3.0 + nrm(ks[5], (DEPTH, n_pool, PAGE_SIZE, N_HEADS_A), 1.0)),
        "page_table": page_table,
        "norm_g": 1.0 + nrm(ks[6], (DEPTH, D_MODEL), 0.05),
        "w_in": nrm(ks[7], (DEPTH, D_MODEL, IN_COLS), D_MODEL ** -0.5),
        "b_forget": 3.0 + nrm(ks[8], (DEPTH, N_HEADS_A), 0.5),
        "q_norm_g": 1.0 + nrm(ks[9], (DEPTH, HEAD_DIM), 0.05),
        "k_norm_g": 1.0 + nrm(ks[10], (DEPTH, HEAD_DIM), 0.05),
        "v_norm_g": 1.0 + nrm(ks[11], (DEPTH, WIDTH_B), 0.05),
        "w_s": nrm(ks[12], (DEPTH, N_GROUPS_B, CHUNK, CHUNK), CHUNK ** -0.5),
        "b_s": 1.0 + nrm(ks[13], (DEPTH, N_GROUPS_B, CHUNK), 0.1),
        "w_proj_a": nrm(ks[14], (DEPTH, WIDTH_A, D_MODEL), WIDTH_A ** -0.5),
        "w_proj_b": nrm(ks[15], (DEPTH, WIDTH_B, D_MODEL), WIDTH_B ** -0.5),
        "w_out": nrm(ks[16], (DEPTH, D_MODEL, D_MODEL), D_MODEL ** -0.5),
    }


def reference(x_prompt, x_sample, cache_k, cache_v, cache_logf, page_table, norm_g, w_in, b_forget,
              q_norm_g, k_norm_g, v_norm_g, w_s, b_s, w_proj_a, w_proj_b, w_out):
    xp, xs = x_prompt, x_sample
    kp_l, vp_l, lp_l, ks_l, vs_l, ls_l, sv_l = [], [], [], [], [], [], []
    for l in range(DEPTH):
        hp = rms_norm(xp, norm_g[l])
        q, k, v, ga, logf, u, vb, gb, g_a, g_b = in_proj(hp, w_in[l], b_forget[l], q_norm_g[l], k_norm_g[l], v_norm_g[l])
        attn = fox_prompt(q, k, v, logf)
        s_sgu = sgu_prompt(vb, w_s[l], b_s[l])
        xp = xp + merge_out(attn, s_sgu, u, ga, gb, g_a, g_b, w_proj_a[l], w_proj_b[l], w_out[l])
        kp_l.append(k)
        vp_l.append(v)
        lp_l.append(logf.astype(cache_logf.dtype))
        hs = rms_norm(xs, norm_g[l])
        q, k, v, ga, logf, u, vb, gb, g_a, g_b = in_proj(hs, w_in[l], b_forget[l], q_norm_g[l], k_norm_g[l], v_norm_g[l])
        attn = fox_sample(q, k, v, logf, cache_k[l], cache_v[l], cache_logf[l], page_table)
        s_sgu = sgu_sample(vb, w_s[l], b_s[l])
        xs = xs + merge_out(attn, s_sgu, u, ga, gb, g_a, g_b, w_proj_a[l], w_proj_b[l], w_out[l])
        ks_l.append(k)
        vs_l.append(v)
        ls_l.append(logf.astype(cache_logf.dtype))
        sv_l.append(vb)
    return (xp, xs, jnp.stack(kp_l), jnp.stack(vp_l), jnp.stack(lp_l),
            jnp.stack(ks_l), jnp.stack(vs_l), jnp.stack(ls_l), jnp.stack(sv_l))
```

```python
import functools
import math

import numpy as np
import jax
import jax.numpy as jnp
from jax import lax
from jax.experimental import pallas as pl
from jax.experimental.pallas import tpu as pltpu

D_MODEL = 1024
N_HEADS = 8
HEAD_DIM = 64
WIDTH_A = N_HEADS * HEAD_DIM
N_GROUPS = 8
GROUP_DIM = 64
WIDTH_B = N_GROUPS * GROUP_DIM
CHUNK = 128
PAGE = 128
EPS = 1e-6
LOG2E = math.log2(math.e)
NEG_BIG = -1e30

LANES = 128
VT_ROWS = 80
VMEM_LIMIT = 56 * 1024 * 1024

C_Q, C_K, C_V, C_GA, C_U, C_VB, C_GB, C_SA, C_SB, C_F, C_END = (
    0, 512, 1024, 1536, 2048, 2560, 3072, 3584, 4608, 5632, 5760)

F32 = jnp.float32
BF16 = jnp.bfloat16


def _split3(x):
    hi = x.astype(BF16)
    r1 = x - hi.astype(F32)
    mid = r1.astype(BF16)
    lo = (r1 - mid.astype(F32)).astype(BF16)
    return hi, mid, lo


def _dot(a, b):
    return jnp.dot(a, b, preferred_element_type=F32)


def _dot_nt(a, b):
    return lax.dot_general(a, b, (((1,), (1,)), ((), ())), preferred_element_type=F32)


def _dot3_right(parts, w):
    return _dot(parts[0], w) + _dot(parts[1], w) + _dot(parts[2], w)


def _dot3_left(w, parts):
    return _dot(w, parts[0]) + _dot(w, parts[1]) + _dot(w, parts[2])


def _sigmoid(x):
    return 1.0 / (1.0 + jnp.exp(-x))


def _gelu_tanh(x):
    c = math.sqrt(2.0 / math.pi)
    return 0.5 * x * (1.0 + jnp.tanh(c * (x + 0.044715 * (x * x * x))))


def _log_sigmoid(x):
    return jnp.minimum(x, 0.0) - jnp.log1p(jnp.exp(-jnp.abs(x)))


def _in_proj_common(x_ref, g_ref, w_ref, bf_ref, qg_ref, kg_ref, vg_ref, bd_ref, wcat_ref, sbias_ref,
                    seq_local):
    tm = x_ref.shape[-2]
    x = x_ref[...].reshape(tm, D_MODEL)
    ms = jnp.mean(x * x, axis=-1, keepdims=True)
    h = ((x * lax.rsqrt(ms + EPS)) * g_ref[...]).astype(BF16)

    def seg(lo, hi):
        return _dot(h, w_ref[:, lo:hi])

    bd = bd_ref[...]

    def head_norm(z, gain):
        msz = _dot((z * z).astype(BF16), bd) * (1.0 / HEAD_DIM)
        return z * lax.rsqrt(msz + EPS) * gain

    qn = head_norm(seg(C_Q, C_K), qg_ref[...])
    kn = head_norm(seg(C_K, C_V), kg_ref[...])
    v = seg(C_V, C_GA)
    zga = seg(C_GA, C_U)
    ga = zga * _sigmoid(zga)

    lane = lax.broadcasted_iota(jnp.int32, (tm, LANES), 1)
    logf = jnp.where(lane < N_HEADS, _log_sigmoid(seg(C_F, C_END) + bf_ref[...]), 0.0)

    u = _gelu_tanh(seg(C_U, C_VB))
    vbg = _gelu_tanh(seg(C_VB, C_GB))
    msv = jnp.mean(vbg * vbg, axis=-1, keepdims=True)
    vbn = vbg * lax.rsqrt(msv + EPS) * vg_ref[...]
    zgb = seg(C_GB, C_SA)
    ugb = u * (zgb * _sigmoid(zgb))

    t_idx = lax.broadcasted_iota(jnp.int32, (CHUNK, N_GROUPS * CHUNK), 0)
    s_idx = lax.broadcasted_iota(jnp.int32, (CHUNK, N_GROUPS * CHUNK), 1) & (CHUNK - 1)
    keep = s_idx <= t_idx
    if seq_local < CHUNK:
        shift = int(math.log2(seq_local))
        keep = keep & ((s_idx >> shift) == (t_idx >> shift))
    wm = jnp.where(keep, wcat_ref[...], jnp.zeros((), BF16))
    glane = lax.broadcasted_iota(jnp.int32, (CHUNK, WIDTH_B), 1) >> 6
    vbn_b = vbn.astype(BF16)
    s_chunks = []
    for c in range(tm // CHUNK):
        vc = vbn_b[c * CHUNK:(c + 1) * CHUNK]
        rhs = jnp.concatenate(
            [jnp.where(glane == g, vc, jnp.zeros((), BF16)) for g in range(N_GROUPS)], axis=0)
        s_chunks.append(_dot(wm, rhs) + sbias_ref[...])
    s_sgu = jnp.concatenate(s_chunks, axis=0)
    bmix = ugb * s_sgu

    sa = _sigmoid(seg(C_SA, C_SB))
    sb = _sigmoid(seg(C_SB, C_F))
    return qn, kn, v, logf, vbn, ga, bmix, sa, sb


def _in_proj_prompt_kernel(x_ref, g_ref, w_ref, bf_ref, qg_ref, kg_ref, vg_ref, bd_ref, wcat_ref,
                           sbias_ref, tri_ref, pk_ref, pq_ref,
                           qt_ref, ka_ref, vt_ref, ko_ref, vo_ref, lf_ref, ga_ref, bm_ref, sa_ref, sb_ref,
                           carry_ref):
    tm = x_ref.shape[-2]
    qn, kn, v, logf, _, ga, bmix, sa, sb = _in_proj_common(
        x_ref, g_ref, w_ref, bf_ref, qg_ref, kg_ref, vg_ref, bd_ref, wcat_ref, sbias_ref, CHUNK)

    ko_ref[0] = kn
    vo_ref[0] = v
    lf_ref[0] = logf[:, :N_HEADS]
    ga_ref[0] = ga.astype(BF16)
    bm_ref[0] = bmix.astype(BF16)
    sa_ref[0] = sa.astype(BF16)
    sb_ref[0] = sb.astype(BF16)

    @pl.when(pl.program_id(1) == 0)
    def _():
        carry_ref[...] = jnp.zeros_like(carry_ref)

    c = _dot3_left(tri_ref[...], _split3(logf)) + carry_ref[0:1, :]
    carry_ref[0:1, :] = c[tm - 1:tm, :]
    c2 = c * LOG2E

    lane = lax.broadcasted_iota(jnp.int32, (tm, LANES), 1)
    hi, mid, lo = _split3(c2)
    cp = (hi.astype(F32) + pltpu.roll(mid.astype(F32), 8, axis=1)
          + pltpu.roll(lo.astype(F32), 16, axis=1) + jnp.where(lane == 24, 1.0, 0.0))
    extra_k = _dot(cp.astype(BF16), pk_ref[...])

    c2t = c2.T
    hit, midt, lot = (part.astype(F32) for part in _split3(c2t[0:8]))
    row = lax.broadcasted_iota(jnp.int32, (8, tm), 0)
    cpt = jnp.concatenate(
        [hit, midt, lot, jnp.where(row == 0, 1.0, 0.0), jnp.zeros((LANES - 32, tm), F32)],
        axis=0).astype(BF16)
    extra_q = _dot(pq_ref[...], cpt)

    q2 = qn * (HEAD_DIM ** -0.5 * LOG2E)
    ones_blk = jnp.where(lax.broadcasted_iota(jnp.int32, (VT_ROWS - HEAD_DIM, tm), 0) == 0, 1.0, 0.0)
    for j in range(N_HEADS // 2):
        qs_t = q2[:, j * LANES:(j + 1) * LANES].T
        vs_t = v[:, j * LANES:(j + 1) * LANES].T
        ks = kn[:, j * LANES:(j + 1) * LANES]
        ks_sw = pltpu.roll(ks, HEAD_DIM, axis=1)
        for hh in range(2):
            hd = 2 * j + hh
            qt_ref[0, hd] = jnp.concatenate(
                [qs_t[hh * HEAD_DIM:(hh + 1) * HEAD_DIM], extra_q[hd * HEAD_DIM:(hd + 1) * HEAD_DIM]],
                axis=0).astype(BF16)
            vt_ref[0, hd, 0] = jnp.concatenate(
                [vs_t[hh * HEAD_DIM:(hh + 1) * HEAD_DIM], ones_blk], axis=0).astype(BF16)
            ka_ref[0, hd] = jnp.where(lane < HEAD_DIM, ks if hh == 0 else ks_sw,
                                      extra_k[:, hd * LANES:(hd + 1) * LANES]).astype(BF16)


def _in_proj_sample_kernel(x_ref, g_ref, w_ref, bf_ref, qg_ref, kg_ref, vg_ref, bd_ref, wcat_ref,
                           sbias_ref,
                           q_ref, ko_ref, vo_ref, lf_ref, lfp_ref, vbn_ref, ga_ref, bm_ref, sa_ref, sb_ref,
                           *, seq_local):
    qn, kn, v, logf, vbn, ga, bmix, sa, sb = _in_proj_common(
        x_ref, g_ref, w_ref, bf_ref, qg_ref, kg_ref, vg_ref, bd_ref, wcat_ref, sbias_ref, seq_local)
    q_ref[...] = qn * (HEAD_DIM ** -0.5 * LOG2E)
    ko_ref[...] = kn
    vo_ref[...] = v
    lf_ref[...] = logf[:, :N_HEADS]
    lfp_ref[...] = logf
    vbn_ref[...] = vbn
    ga_ref[...] = ga.astype(BF16)
    bm_ref[...] = bmix.astype(BF16)
    sa_ref[...] = sa.astype(BF16)
    sb_ref[...] = sb.astype(BF16)


def _const_spec(shape):
    return pl.BlockSpec(shape, lambda *_: (0,) * len(shape))


def _in_proj_consts(tm):
    bd = np.kron(np.eye(N_HEADS, dtype=np.float32), np.ones((HEAD_DIM, HEAD_DIM), np.float32))
    tri = np.tril(np.ones((tm, tm), np.float32))
    pk = np.zeros((LANES, N_HEADS * LANES), np.float32)
    pq = np.zeros((N_HEADS * HEAD_DIM, LANES), np.float32)
    for hd in range(N_HEADS):
        for part in range(3):
            pk[24, hd * LANES + HEAD_DIM + part] = 1.0
            pk[8 * part + hd, hd * LANES + HEAD_DIM + 3 + part] = -1.0
            pq[hd * HEAD_DIM + part, 8 * part + hd] = 1.0
            pq[hd * HEAD_DIM + 3 + part, 24] = 1.0
    return (jnp.asarray(bd, BF16), jnp.asarray(tri, BF16), jnp.asarray(pk, BF16), jnp.asarray(pq, BF16))


def _in_proj_prompt(x, params, tm):
    b, s, _ = x.shape
    nt = s // tm
    bd, tri, pk, pq = _in_proj_consts(tm)
    g, w, bf, qg, kg, vg, wcat, sbias = params
    consts = (g, w, bf, qg, kg, vg, bd, wcat, sbias, tri, pk, pq)
    row3 = lambda width: pl.BlockSpec((1, tm, width), lambda bi, i: (bi, i, 0))
    out_shape = (
        jax.ShapeDtypeStruct((b, N_HEADS, LANES, s), BF16),
        jax.ShapeDtypeStruct((b, N_HEADS, s, LANES), BF16),
        jax.ShapeDtypeStruct((b, N_HEADS, nt, VT_ROWS, tm), BF16),
        jax.ShapeDtypeStruct((b, s, WIDTH_A), F32),
        jax.ShapeDtypeStruct((b, s, WIDTH_A), F32),
        jax.ShapeDtypeStruct((b, s, N_HEADS), F32),
        jax.ShapeDtypeStruct((b, s, WIDTH_A), BF16),
        jax.ShapeDtypeStruct((b, s, WIDTH_B), BF16),
        jax.ShapeDtypeStruct((b, s, D_MODEL), BF16),
        jax.ShapeDtypeStruct((b, s, D_MODEL), BF16),
    )
    out_specs = (
        pl.BlockSpec((1, N_HEADS, LANES, tm), lambda bi, i: (bi, 0, 0, i)),
        pl.BlockSpec((1, N_HEADS, tm, LANES), lambda bi, i: (bi, 0, i, 0)),
        pl.BlockSpec((1, N_HEADS, 1, VT_ROWS, tm), lambda bi, i: (bi, 0, i, 0, 0)),
        row3(WIDTH_A), row3(WIDTH_A), row3(N_HEADS), row3(WIDTH_A), row3(WIDTH_B),
        row3(D_MODEL), row3(D_MODEL),
    )
    return pl.pallas_call(
        _in_proj_prompt_kernel,
        out_shape=out_shape,
        grid=(b, nt),
        in_specs=[row3(D_MODEL)] + [_const_spec(c.shape) for c in consts],
        out_specs=out_specs,
        scratch_shapes=[pltpu.VMEM((8, LANES), F32)],
        compiler_params=pltpu.CompilerParams(
            dimension_semantics=("arbitrary", "arbitrary"), vmem_limit_bytes=VMEM_LIMIT),
        name="in_proj_prompt",
    )(x, *consts)


def _in_proj_sample(x, params, tm, seq_local):
    n = x.shape[0]
    bd = _in_proj_consts(tm)[0]
    g, w, bf, qg, kg, vg, wcat, sbias = params
    consts = (g, w, bf, qg, kg, vg, bd, wcat, sbias)
    row2 = lambda width: pl.BlockSpec((tm, width), lambda i: (i, 0))
    out_shape = (
        jax.ShapeDtypeStruct((n, WIDTH_A), F32),
        jax.ShapeDtypeStruct((n, WIDTH_A), F32),
        jax.ShapeDtypeStruct((n, WIDTH_A), F32),
        jax.ShapeDtypeStruct((n, N_HEADS), F32),
        jax.ShapeDtypeStruct((n, LANES), F32),
        jax.ShapeDtypeStruct((n, WIDTH_B), F32),
        jax.ShapeDtypeStruct((n, WIDTH_A), BF16),
        jax.ShapeDtypeStruct((n, WIDTH_B), BF16),
        jax.ShapeDtypeStruct((n, D_MODEL), BF16),
        jax.ShapeDtypeStruct((n, D_MODEL), BF16),
    )
    out_specs = (row2(WIDTH_A), row2(WIDTH_A), row2(WIDTH_A), row2(N_HEADS), row2(LANES), row2(WIDTH_B),
                 row2(WIDTH_A), row2(WIDTH_B), row2(D_MODEL), row2(D_MODEL))
    return pl.pallas_call(
        functools.partial(_in_proj_sample_kernel, seq_local=seq_local),
        out_shape=out_shape,
        grid=(n // tm,),
        in_specs=[row2(D_MODEL)] + [_const_spec(c.shape) for c in consts],
        out_specs=out_specs,
        compiler_params=pltpu.CompilerParams(
            dimension_semantics=("arbitrary",), vmem_limit_bytes=VMEM_LIMIT),
        name="in_proj_sample",
    )(x, *consts)


def _attn_prompt_kernel(qt_ref, ka_ref, vt_ref, o_ref, m_ref, acc_ref):
    t = qt_ref.shape[-1]
    i = pl.program_id(2)
    qt = qt_ref[0, 0]
    m_ref[...] = jnp.full_like(m_ref, NEG_BIG)
    acc_ref[...] = jnp.zeros_like(acc_ref)

    def step(j, masked):
        k = ka_ref[0, 0, pl.ds(pl.multiple_of(j * t, t), t), :]
        s = _dot(k, qt)
        if masked:
            key = lax.broadcasted_iota(jnp.int32, (t, t), 0)
            qry = lax.broadcasted_iota(jnp.int32, (t, t), 1)
            s = jnp.where(key <= qry, s, NEG_BIG)
        m_old = m_ref[0:1, :]
        m_new = jnp.maximum(m_old, jnp.max(s, axis=0, keepdims=True))
        p = jnp.exp2(s - m_new).astype(BF16)
        alpha = jnp.exp2(m_old - m_new)
        acc_ref[...] = alpha * acc_ref[...] + _dot(vt_ref[0, 0, j], p)
        m_ref[0:1, :] = m_new

    def body(j, carry):
        step(j, False)
        return carry

    lax.fori_loop(0, i, body, 0)
    step(i, True)
    acc = acc_ref[...]
    o_ref[0] = acc[0:HEAD_DIM] / acc[HEAD_DIM:HEAD_DIM + 1]


def _attn_prompt(qt, ka, vt):
    b, h, _, s = qt.shape
    t = vt.shape[-1]
    nq = s // t
    return pl.pallas_call(
        _attn_prompt_kernel,
        out_shape=jax.ShapeDtypeStruct((b, WIDTH_A, s), F32),
        grid=(b, h, nq),
        in_specs=[
            pl.BlockSpec((1, 1, LANES, t), lambda bi, hi, i: (bi, hi, 0, i)),
            pl.BlockSpec((1, 1, s, LANES), lambda bi, hi, i: (bi, hi, 0, 0)),
            pl.BlockSpec((1, 1, nq, VT_ROWS, t), lambda bi, hi, i: (bi, hi, 0, 0, 0)),
        ],
        out_specs=pl.BlockSpec((1, HEAD_DIM, t), lambda bi, hi, i: (bi, hi, i)),
        scratch_shapes=[pltpu.VMEM((8, t), F32), pltpu.VMEM((VT_ROWS, t), F32)],
        compiler_params=pltpu.CompilerParams(
            dimension_semantics=("arbitrary", "arbitrary", "arbitrary"), vmem_limit_bytes=VMEM_LIMIT),
        name="attn_prompt",
    )(qt, ka, vt)


def _attn_sample_kernel(n_pages, t_new, pt_ref, q_ref, kn_ref, vn_ref, ln_ref, ls_ref, ones_ref, m3_ref,
                        eh_ref, ehh_ref, *rest):
    del pt_ref
    k_refs = rest[:n_pages]
    v_refs = rest[n_pages:2 * n_pages]
    l_refs = rest[2 * n_pages:3 * n_pages]
    o_ref = rest[3 * n_pages]
    rows = N_HEADS * t_new

    q = q_ref[...]
    row = lax.broadcasted_iota(jnp.int32, (rows, WIDTH_A), 0)
    lane = lax.broadcasted_iota(jnp.int32, (rows, WIDTH_A), 1)
    qbd = jnp.where((row >> 3) == (lane >> 6), jnp.concatenate([q] * N_HEADS, axis=0), 0.0).astype(BF16)

    cn = ln_ref[...] * LOG2E
    trow = lax.broadcasted_iota(jnp.int32, (t_new, LANES), 0)
    sh = 1
    while sh < t_new:
        cn = cn + jnp.where(trow >= sh, pltpu.roll(cn, sh, axis=0), 0.0)
        sh *= 2
    row1 = lax.broadcasted_iota(jnp.int32, (rows, LANES), 0)
    lane1 = lax.broadcasted_iota(jnp.int32, (rows, LANES), 1)
    cn_rows = jnp.concatenate([cn] * N_HEADS, axis=0)
    cn_col = jnp.sum(jnp.where((row1 >> 3) == lane1, cn_rows, 0.0), axis=1, keepdims=True)

    x = jnp.concatenate([l_refs[p][0, 0] for p in range(n_pages)], axis=0) * LOG2E
    xp = _split3(x)
    r_in = _dot3_right(xp, ls_ref[...])
    tot_b = _dot3_right(xp, ones_ref[...])
    r_page = _dot3_left(m3_ref[...], _split3(tot_b))
    rp = _split3(r_in + r_page)
    bias_past = _dot3_left(eh_ref[...], rp)

    s_blocks = []
    for p in range(n_pages):
        kp = k_refs[p][0, 0].astype(BF16)
        s_blocks.append(_dot_nt(qbd, kp) + bias_past[p * rows:(p + 1) * rows] + cn_col)
    pad = jnp.zeros((PAGE - t_new, WIDTH_A), F32)
    kn = jnp.concatenate([kn_ref[...], pad], axis=0).astype(BF16)
    vn = jnp.concatenate([vn_ref[...], pad], axis=0).astype(BF16)
    cn_pad = jnp.concatenate([cn, jnp.zeros((PAGE - t_new, LANES), F32)], axis=0)
    cnp = _split3(cn_pad)
    ehh = ehh_ref[...]
    cn_keys = _dot_nt(ehh, cnp[0]) + _dot_nt(ehh, cnp[1]) + _dot_nt(ehh, cnp[2])
    s_new = _dot_nt(qbd, kn) + cn_col - cn_keys
    s_new = jnp.where((lane1 <= (row1 & (t_new - 1))) & (lane1 < t_new), s_new, NEG_BIG)
    s_blocks.append(s_new)

    m = s_blocks[0]
    for sb in s_blocks[1:]:
        m = jnp.maximum(m, sb)
    m = jnp.max(m, axis=1, keepdims=True)
    l_sum = jnp.zeros((rows, 1), F32)
    o = jnp.zeros((rows, WIDTH_A), F32)
    for p in range(n_pages + 1):
        pe = jnp.exp2(s_blocks[p] - m)
        l_sum = l_sum + jnp.sum(pe, axis=1, keepdims=True)
        vp = v_refs[p][0, 0].astype(BF16) if p < n_pages else vn
        o = o + _dot(pe.astype(BF16), vp)
    o = o / l_sum
    lane_h = lax.broadcasted_iota(jnp.int32, (t_new, WIDTH_A), 1) >> 6
    out = jnp.zeros((t_new, WIDTH_A), F32)
    for hd in range(N_HEADS):
        out = out + jnp.where(lane_h == hd, o[hd * t_new:(hd + 1) * t_new], 0.0)
    o_ref[...] = out


def _attn_sample(layer, page_table, q, kn, vn, lnp, ck, cv, clt, t_new):
    n_seq, n_pages = page_table.shape
    rows = N_HEADS * t_new
    pr = n_pages * N_HEADS
    key = np.arange(PAGE)
    ls = (key[:, None] > key[None, :]).astype(np.float32)
    ones = np.ones((PAGE, LANES), np.float32)
    ph = np.arange(pr)
    m3 = ((ph[:, None] % N_HEADS == ph[None, :] % N_HEADS)
          & (ph[None, :] // N_HEADS > ph[:, None] // N_HEADS)).astype(np.float32)
    r_all = np.arange(n_pages * rows)
    eh = ((r_all[:, None] // rows == ph[None, :] // N_HEADS)
          & ((r_all[:, None] % rows) // t_new == ph[None, :] % N_HEADS)).astype(np.float32)
    ehh = ((np.arange(rows)[:, None] // t_new) == np.arange(LANES)[None, :]).astype(np.float32)
    consts = tuple(jnp.asarray(a, BF16) for a in (ls, ones, m3, eh, ehh))

    tok = lambda width: pl.BlockSpec((t_new, width), lambda b, pt: (b, 0))
    def page_specs(block):
        return [pl.BlockSpec(block, functools.partial(lambda b, pt, p: (layer, pt[b, p], 0, 0), p=p))
                for p in range(n_pages)]
    grid_spec = pltpu.PrefetchScalarGridSpec(
        num_scalar_prefetch=1,
        grid=(n_seq,),
        in_specs=[tok(WIDTH_A), tok(WIDTH_A), tok(WIDTH_A), tok(LANES)]
        + [pl.BlockSpec(c.shape, lambda b, pt: (0, 0)) for c in consts]
        + page_specs((1, 1, PAGE, WIDTH_A)) + page_specs((1, 1, PAGE, WIDTH_A))
        + page_specs((1, 1, N_HEADS, PAGE)),
        out_specs=tok(WIDTH_A),
    )
    return pl.pallas_call(
        functools.partial(_attn_sample_kernel, n_pages, t_new),
        out_shape=jax.ShapeDtypeStruct((n_seq * t_new, WIDTH_A), F32),
        grid_spec=grid_spec,
        compiler_params=pltpu.CompilerParams(
            dimension_semantics=("arbitrary",), vmem_limit_bytes=VMEM_LIMIT),
        name="attn_sample",
    )(page_table, q, kn, vn, lnp, *consts, *([ck] * n_pages), *([cv] * n_pages), *([clt] * n_pages))


def _merge_kernel(transposed, a_ref, ga_ref, bm_ref, sa_ref, sb_ref, x_ref, wpa_ref, wpb_ref, wo_ref, y_ref):
    tm = x_ref.shape[-2]
    attn = a_ref[0].T if transposed else a_ref[...]
    ga = ga_ref[...].reshape(tm, WIDTH_A)
    a = (attn * ga.astype(F32)).astype(BF16)
    pa = _dot(a, wpa_ref[...])
    pb = _dot(bm_ref[...].reshape(tm, WIDTH_B), wpb_ref[...])
    m = sa_ref[...].reshape(tm, D_MODEL).astype(F32) * pa + sb_ref[...].reshape(tm, D_MODEL).astype(F32) * pb
    y = x_ref[...].reshape(tm, D_MODEL) + _dot(m.astype(BF16), wo_ref[...])
    y_ref[...] = y.reshape(y_ref.shape)


def _merge_prompt(attn_t, ga, bm, sa, sb, x, wpa, wpb, wo, tm):
    b, s, _ = x.shape
    row3 = lambda width: pl.BlockSpec((1, tm, width), lambda bi, i: (bi, i, 0))
    return pl.pallas_call(
        functools.partial(_merge_kernel, True),
        out_shape=jax.ShapeDtypeStruct((b, s, D_MODEL), F32),
        grid=(b, s // tm),
        in_specs=[pl.BlockSpec((1, WIDTH_A, tm), lambda bi, i: (bi, 0, i)),
                  row3(WIDTH_A), row3(WIDTH_B), row3(D_MODEL), row3(D_MODEL), row3(D_MODEL),
                  _const_spec(wpa.shape), _const_spec(wpb.shape), _const_spec(wo.shape)],
        out_specs=row3(D_MODEL),
        compiler_params=pltpu.CompilerParams(
            dimension_semantics=("arbitrary", "arbitrary"), vmem_limit_bytes=VMEM_LIMIT),
        name="merge_prompt",
    )(attn_t, ga, bm, sa, sb, x, wpa, wpb, wo)


def _merge_sample(attn, ga, bm, sa, sb, x, wpa, wpb, wo, tm):
    n = x.shape[0]
    row2 = lambda width: pl.BlockSpec((tm, width), lambda i: (i, 0))
    return pl.pallas_call(
        functools.partial(_merge_kernel, False),
        out_shape=jax.ShapeDtypeStruct((n, D_MODEL), F32),
        grid=(n // tm,),
        in_specs=[row2(WIDTH_A), row2(WIDTH_A), row2(WIDTH_B), row2(D_MODEL), row2(D_MODEL), row2(D_MODEL),
                  _const_spec(wpa.shape), _const_spec(wpb.shape), _const_spec(wo.shape)],
        out_specs=row2(D_MODEL),
        compiler_params=pltpu.CompilerParams(
            dimension_semantics=("arbitrary",), vmem_limit_bytes=VMEM_LIMIT),
        name="merge_sample",
    )(attn, ga, bm, sa, sb, x, wpa, wpb, wo)


TM_PROMPT = 256
TM_MERGE = 512
TM_SAMPLE = 256


def kernel(x_prompt, x_sample, cache_k, cache_v, cache_logf, page_table, norm_g, w_in, b_forget, q_norm_g,
           k_norm_g, v_norm_g, w_s, b_s, w_proj_a, w_proj_b, w_out):
    depth = w_in.shape[0]
    bsz, seq, _ = x_prompt.shape
    n_seq, t_new, _ = x_sample.shape
    n_pool = cache_k.shape[1]

    f_lo = 4 * WIDTH_A
    w_perm = jnp.concatenate(
        [w_in[:, :, :f_lo], w_in[:, :, f_lo + N_HEADS:], w_in[:, :, f_lo:f_lo + N_HEADS],
         jnp.zeros((depth, D_MODEL, C_END - C_F - N_HEADS), w_in.dtype)], axis=-1).astype(BF16)
    bf_pad = jnp.pad(b_forget, ((0, 0), (0, LANES - N_HEADS)))[:, None, :]
    qg = jnp.tile(q_norm_g, (1, N_HEADS))[:, None, :]
    kg = jnp.tile(k_norm_g, (1, N_HEADS))[:, None, :]
    vg = v_norm_g[:, None, :]
    ng = norm_g[:, None, :]
    wcat_p = w_s.transpose(0, 2, 1, 3).reshape(depth, CHUNK, N_GROUPS * CHUNK).astype(BF16)
    w_small = jnp.tile(w_s[:, :, :t_new, :t_new], (1, 1, CHUNK // t_new, CHUNK // t_new))
    wcat_s = w_small.transpose(0, 2, 1, 3).reshape(depth, CHUNK, N_GROUPS * CHUNK).astype(BF16)
    sbias_p = jnp.repeat(b_s.transpose(0, 2, 1), GROUP_DIM, axis=-1)
    sbias_s = jnp.repeat(jnp.tile(b_s[:, :, :t_new], (1, 1, CHUNK // t_new)).transpose(0, 2, 1),
                         GROUP_DIM, axis=-1)
    wpa = w_proj_a.astype(BF16)
    wpb = w_proj_b.astype(BF16)
    wo = w_out.astype(BF16)
    ck = cache_k.reshape(depth, n_pool, PAGE, WIDTH_A)
    cv = cache_v.reshape(depth, n_pool, PAGE, WIDTH_A)
    clt = cache_logf.transpose(0, 1, 3, 2)

    xp = x_prompt
    xs = x_sample.reshape(n_seq * t_new, D_MODEL)
    outs = [[] for _ in range(7)]
    for l in range(depth):
        common = (ng[l], w_perm[l], bf_pad[l], qg[l], kg[l], vg[l])
        qt, ka, vt, k_new, v_new, logf, ga, bm, sa, sb = _in_proj_prompt(
            xp, common + (wcat_p[l], sbias_p[l]), TM_PROMPT)
        attn_t = _attn_prompt(qt, ka, vt)
        xp = _merge_prompt(attn_t, ga, bm, sa, sb, xp, wpa[l], wpb[l], wo[l], TM_MERGE)
        outs[0].append(k_new.reshape(bsz, seq, N_HEADS, HEAD_DIM))
        outs[1].append(v_new.reshape(bsz, seq, N_HEADS, HEAD_DIM))
        outs[2].append(logf)

        q_s, k_s, v_s, lf_s, lfp_s, vbn_s, ga, bm, sa, sb = _in_proj_sample(
            xs, common + (wcat_s[l], sbias_s[l]), TM_SAMPLE, t_new)
        attn_s = _attn_sample(l, page_table, q_s, k_s, v_s, lfp_s, ck, cv, clt, t_new)
        xs = _merge_sample(attn_s, ga, bm, sa, sb, xs, wpa[l], wpb[l], wo[l], TM_SAMPLE)
        outs[3].append(k_s.reshape(n_seq, t_new, N_HEADS, HEAD_DIM))
        outs[4].append(v_s.reshape(n_seq, t_new, N_HEADS, HEAD_DIM))
        outs[5].append(lf_s.reshape(n_seq, t_new, N_HEADS))
        outs[6].append(vbn_s.reshape(n_seq, t_new, WIDTH_B))
    return (xp, xs.reshape(n_seq, t_new, D_MODEL)) + tuple(jnp.stack(o) for o in outs)
```

```python
import functools
import math

import numpy as np
import jax
import jax.numpy as jnp
from jax import lax
from jax.experimental import pallas as pl
from jax.experimental.pallas import tpu as pltpu

D_MODEL = 1024
N_HEADS = 8
HEAD_DIM = 64
WIDTH_A = N_HEADS * HEAD_DIM
N_GROUPS = 8
GROUP_DIM = 64
WIDTH_B = N_GROUPS * GROUP_DIM
CHUNK = 128
PAGE = 128
EPS = 1e-6
LOG2E = math.log2(math.e)
NEG_BIG = -1e30

LANES = 128
VT_ROWS = 80
VMEM_LIMIT = 56 * 1024 * 1024

C_Q, C_K, C_V, C_GA, C_U, C_VB, C_GB, C_SA, C_SB, C_F, C_END = (
    0, 512, 1024, 1536, 2048, 2560, 3072, 3584, 4608, 5632, 5760)

F32 = jnp.float32
BF16 = jnp.bfloat16


def _split3(x):
    hi = x.astype(BF16)
    r1 = x - hi.astype(F32)
    mid = r1.astype(BF16)
    lo = (r1 - mid.astype(F32)).astype(BF16)
    return hi, mid, lo


def _dot(a, b):
    return jnp.dot(a, b, preferred_element_type=F32)


def _dot_nt(a, b):
    return lax.dot_general(a, b, (((1,), (1,)), ((), ())), preferred_element_type=F32)


def _dot3_right(parts, w):
    return _dot(parts[0], w) + _dot(parts[1], w) + _dot(parts[2], w)


def _dot3_left(w, parts):
    return _dot(w, parts[0]) + _dot(w, parts[1]) + _dot(w, parts[2])


def _sigmoid(x):
    return 1.0 / (1.0 + jnp.exp(-x))


def _gelu_tanh(x):
    c = math.sqrt(2.0 / math.pi)
    return 0.5 * x * (1.0 + jnp.tanh(c * (x + 0.044715 * (x * x * x))))


def _log_sigmoid(x):
    return jnp.minimum(x, 0.0) - jnp.log1p(jnp.exp(-jnp.abs(x)))


def _in_proj_common(x_ref, g_ref, w_ref, bf_ref, qg_ref, kg_ref, vg_ref, bd_ref, wcat_ref, sbias_ref,
                    seq_local):
    tm = x_ref.shape[-2]
    x = x_ref[...].reshape(tm, D_MODEL)
    ms = jnp.mean(x * x, axis=-1, keepdims=True)
    h = ((x * lax.rsqrt(ms + EPS)) * g_ref[...]).astype(BF16)

    def seg(lo, hi):
        return _dot(h, w_ref[:, lo:hi])

    bd = bd_ref[...]

    def head_norm(z, gain):
        msz = _dot((z * z).astype(BF16), bd) * (1.0 / HEAD_DIM)
        return z * lax.rsqrt(msz + EPS) * gain

    qn = head_norm(seg(C_Q, C_K), qg_ref[...])
    kn = head_norm(seg(C_K, C_V), kg_ref[...])
    v = seg(C_V, C_GA)
    zga = seg(C_GA, C_U)
    ga = zga * _sigmoid(zga)

    lane = lax.broadcasted_iota(jnp.int32, (tm, LANES), 1)
    logf = jnp.where(lane < N_HEADS, _log_sigmoid(seg(C_F, C_END) + bf_ref[...]), 0.0)

    u = _gelu_tanh(seg(C_U, C_VB))
    vbg = _gelu_tanh(seg(C_VB, C_GB))
    msv = jnp.mean(vbg * vbg, axis=-1, keepdims=True)
    vbn = vbg * lax.rsqrt(msv + EPS) * vg_ref[...]
    zgb = seg(C_GB, C_SA)
    ugb = u * (zgb * _sigmoid(zgb))

    t_idx = lax.broadcasted_iota(jnp.int32, (CHUNK, N_GROUPS * CHUNK), 0)
    s_idx = lax.broadcasted_iota(jnp.int32, (CHUNK, N_GROUPS * CHUNK), 1) & (CHUNK - 1)
    keep = s_idx <= t_idx
    if seq_local < CHUNK:
        shift = int(math.log2(seq_local))
        keep = keep & ((s_idx >> shift) == (t_idx >> shift))
    wm = jnp.where(keep, wcat_ref[...], jnp.zeros((), BF16))
    glane = lax.broadcasted_iota(jnp.int32, (CHUNK, WIDTH_B), 1) >> 6
    vbn_b = vbn.astype(BF16)
    s_chunks = []
    for c in range(tm // CHUNK):
        vc = vbn_b[c * CHUNK:(c + 1) * CHUNK]
        rhs = jnp.concatenate(
            [jnp.where(glane == g, vc, jnp.zeros((), BF16)) for g in range(N_GROUPS)], axis=0)
        s_chunks.append(_dot(wm, rhs) + sbias_ref[...])
    s_sgu = jnp.concatenate(s_chunks, axis=0)
    bmix = ugb * s_sgu

    sa = _sigmoid(seg(C_SA, C_SB))
    sb = _sigmoid(seg(C_SB, C_F))
    return qn, kn, v, logf, vbn, ga, bmix, sa, sb


def _in_proj_prompt_kernel(x_ref, g_ref, w_ref, bf_ref, qg_ref, kg_ref, vg_ref, bd_ref, wcat_ref,
                           sbias_ref, tri_ref, pk_ref, pq_ref,
                           qt_ref, ka_ref, vt_ref, ko_ref, vo_ref, lf_ref, ga_ref, bm_ref, sa_ref, sb_ref,
                           carry_ref):
    tm = x_ref.shape[-2]
    qn, kn, v, logf, _, ga, bmix, sa, sb = _in_proj_common(
        x_ref, g_ref, w_ref, bf_ref, qg_ref, kg_ref, vg_ref, bd_ref, wcat_ref, sbias_ref, CHUNK)

    ko_ref[0] = kn
    vo_ref[0] = v
    lf_ref[0] = logf[:, :N_HEADS]
    ga_ref[0] = ga.astype(BF16)
    bm_ref[0] = bmix.astype(BF16)
    sa_ref[0] = sa.astype(BF16)
    sb_ref[0] = sb.astype(BF16)

    @pl.when(pl.program_id(1) == 0)
    def _():
        carry_ref[...] = jnp.zeros_like(carry_ref)

    c = _dot3_left(tri_ref[...], _split3(logf)) + carry_ref[0:1, :]
    carry_ref[0:1, :] = c[tm - 1:tm, :]
    c2 = c * LOG2E

    lane = lax.broadcasted_iota(jnp.int32, (tm, LANES), 1)
    hi, mid, lo = _split3(c2)
    cp = (hi.astype(F32) + pltpu.roll(mid.astype(F32), 8, axis=1)
          + pltpu.roll(lo.astype(F32), 16, axis=1) + jnp.where(lane == 24, 1.0, 0.0))
    extra_k = _dot(cp.astype(BF16), pk_ref[...])

    c2t = c2.T
    hit, midt, lot = (part.astype(F32) for part in _split3(c2t[0:8]))
    row = lax.broadcasted_iota(jnp.int32, (8, tm), 0)
    cpt = jnp.concatenate(
        [hit, midt, lot, jnp.where(row == 0, 1.0, 0.0), jnp.zeros((LANES - 32, tm), F32)],
        axis=0).astype(BF16)
    extra_q = _dot(pq_ref[...], cpt)

    q2 = qn * (HEAD_DIM ** -0.5 * LOG2E)
    ones_blk = jnp.where(lax.broadcasted_iota(jnp.int32, (VT_ROWS - HEAD_DIM, tm), 0) == 0, 1.0, 0.0)
    for j in range(N_HEADS // 2):
        qs_t = q2[:, j * LANES:(j + 1) * LANES].T
        vs_t = v[:, j * LANES:(j + 1) * LANES].T
        ks = kn[:, j * LANES:(j + 1) * LANES]
        ks_sw = pltpu.roll(ks, HEAD_DIM, axis=1)
        for hh in range(2):
            hd = 2 * j + hh
            qt_ref[0, hd] = jnp.concatenate(
                [qs_t[hh * HEAD_DIM:(hh + 1) * HEAD_DIM], extra_q[hd * HEAD_DIM:(hd + 1) * HEAD_DIM]],
                axis=0).astype(BF16)
            vt_ref[0, hd, 0] = jnp.concatenate(
                [vs_t[hh * HEAD_DIM:(hh + 1) * HEAD_DIM], ones_blk], axis=0).astype(BF16)
            ka_ref[0, hd] = jnp.where(lane < HEAD_DIM, ks if hh == 0 else ks_sw,
                                      extra_k[:, hd * LANES:(hd + 1) * LANES]).astype(BF16)


def _in_proj_sample_kernel(x_ref, g_ref, w_ref, bf_ref, qg_ref, kg_ref, vg_ref, bd_ref, wcat_ref,
                           sbias_ref,
                           q_ref, ko_ref, vo_ref, lf_ref, lfp_ref, vbn_ref, ga_ref, bm_ref, sa_ref, sb_ref,
                           *, seq_local):
    qn, kn, v, logf, vbn, ga, bmix, sa, sb = _in_proj_common(
        x_ref, g_ref, w_ref, bf_ref, qg_ref, kg_ref, vg_ref, bd_ref, wcat_ref, sbias_ref, seq_local)
    q_ref[...] = qn * (HEAD_DIM ** -0.5 * LOG2E)
    ko_ref[...] = kn
    vo_ref[...] = v
    lf_ref[...] = logf[:, :N_HEADS]
    lfp_ref[...] = logf
    vbn_ref[...] = vbn
    ga_ref[...] = ga.astype(BF16)
    bm_ref[...] = bmix.astype(BF16)
    sa_ref[...] = sa.astype(BF16)
    sb_ref[...] = sb.astype(BF16)


def _const_spec(shape):
    return pl.BlockSpec(shape, lambda *_: (0,) * len(shape))


def _in_proj_consts(tm):
    bd = np.kron(np.eye(N_HEADS, dtype=np.float32), np.ones((HEAD_DIM, HEAD_DIM), np.float32))
    tri = np.tril(np.ones((tm, tm), np.float32))
    pk = np.zeros((LANES, N_HEADS * LANES), np.float32)
    pq = np.zeros((N_HEADS * HEAD_DIM, LANES), np.float32)
    for hd in range(N_HEADS):
        for part in range(3):
            pk[24, hd * LANES + HEAD_DIM + part] = 1.0
            pk[8 * part + hd, hd * LANES + HEAD_DIM + 3 + part] = -1.0
            pq[hd * HEAD_DIM + part, 8 * part + hd] = 1.0
            pq[hd * HEAD_DIM + 3 + part, 24] = 1.0
    return (jnp.asarray(bd, BF16), jnp.asarray(tri, BF16), jnp.asarray(pk, BF16), jnp.asarray(pq, BF16))


def _in_proj_prompt(x, params, tm):
    b, s, _ = x.shape
    nt = s // tm
    bd, tri, pk, pq = _in_proj_consts(tm)
    g, w, bf, qg, kg, vg, wcat, sbias = params
    consts = (g, w, bf, qg, kg, vg, bd, wcat, sbias, tri, pk, pq)
    row3 = lambda width: pl.BlockSpec((1, tm, width), lambda bi, i: (bi, i, 0))
    out_shape = (
        jax.ShapeDtypeStruct((b, N_HEADS, LANES, s), BF16),
        jax.ShapeDtypeStruct((b, N_HEADS, s, LANES), BF16),
        jax.ShapeDtypeStruct((b, N_HEADS, nt, VT_ROWS, tm), BF16),
        jax.ShapeDtypeStruct((b, s, WIDTH_A), F32),
        jax.ShapeDtypeStruct((b, s, WIDTH_A), F32),
        jax.ShapeDtypeStruct((b, s, N_HEADS), F32),
        jax.ShapeDtypeStruct((b, s, WIDTH_A), BF16),
        jax.ShapeDtypeStruct((b, s, WIDTH_B), BF16),
        jax.ShapeDtypeStruct((b, s, D_MODEL), BF16),
        jax.ShapeDtypeStruct((b, s, D_MODEL), BF16),
    )
    out_specs = (
        pl.BlockSpec((1, N_HEADS, LANES, tm), lambda bi, i: (bi, 0, 0, i)),
        pl.BlockSpec((1, N_HEADS, tm, LANES), lambda bi, i: (bi, 0, i, 0)),
        pl.BlockSpec((1, N_HEADS, 1, VT_ROWS, tm), lambda bi, i: (bi, 0, i, 0, 0)),
        row3(WIDTH_A), row3(WIDTH_A), row3(N_HEADS), row3(WIDTH_A), row3(WIDTH_B),
        row3(D_MODEL), row3(D_MODEL),
    )
    return pl.pallas_call(
        _in_proj_prompt_kernel,
        out_shape=out_shape,
        grid=(b, nt),
        in_specs=[row3(D_MODEL)] + [_const_spec(c.shape) for c in consts],
        out_specs=out_specs,
        scratch_shapes=[pltpu.VMEM((8, LANES), F32)],
        compiler_params=pltpu.CompilerParams(
            dimension_semantics=("arbitrary", "arbitrary"), vmem_limit_bytes=VMEM_LIMIT),
        name="in_proj_prompt",
    )(x, *consts)


def _in_proj_sample(x, params, tm, seq_local):
    n = x.shape[0]
    bd = _in_proj_consts(tm)[0]
    g, w, bf, qg, kg, vg, wcat, sbias = params
    consts = (g, w, bf, qg, kg, vg, bd, wcat, sbias)
    row2 = lambda width: pl.BlockSpec((tm, width), lambda i: (i, 0))
    out_shape = (
        jax.ShapeDtypeStruct((n, WIDTH_A), F32),
        jax.ShapeDtypeStruct((n, WIDTH_A), F32),
        jax.ShapeDtypeStruct((n, WIDTH_A), F32),
        jax.ShapeDtypeStruct((n, N_HEADS), F32),
        jax.ShapeDtypeStruct((n, LANES), F32),
        jax.ShapeDtypeStruct((n, WIDTH_B), F32),
        jax.ShapeDtypeStruct((n, WIDTH_A), BF16),
        jax.ShapeDtypeStruct((n, WIDTH_B), BF16),
        jax.ShapeDtypeStruct((n, D_MODEL), BF16),
        jax.ShapeDtypeStruct((n, D_MODEL), BF16),
    )
    out_specs = (row2(WIDTH_A), row2(WIDTH_A), row2(WIDTH_A), row2(N_HEADS), row2(LANES), row2(WIDTH_B),
                 row2(WIDTH_A), row2(WIDTH_B), row2(D_MODEL), row2(D_MODEL))
    return pl.pallas_call(
        functools.partial(_in_proj_sample_kernel, seq_local=seq_local),
        out_shape=out_shape,
        grid=(n // tm,),
        in_specs=[row2(D_MODEL)] + [_const_spec(c.shape) for c in consts],
        out_specs=out_specs,
        compiler_params=pltpu.CompilerParams(
            dimension_semantics=("arbitrary",), vmem_limit_bytes=VMEM_LIMIT),
        name="in_proj_sample",
    )(x, *consts)


def _attn_prompt_kernel(qt_ref, ka_ref, vt_ref, o_ref, s_ref, p_ref, m_ref, al_ref, acc_ref):
    g = qt_ref.shape[1]
    t = qt_ref.shape[-1]
    i = pl.program_id(2)
    key = lax.broadcasted_iota(jnp.int32, (t, t), 0)
    qry = lax.broadcasted_iota(jnp.int32, (t, t), 1)

    def scores(blk, slot, mask, m_prev):
        for hd in range(g):
            k = ka_ref[0, hd, pl.ds(pl.multiple_of(blk * t, t), t), :]
            s = _dot(k, qt_ref[0, hd])
            if mask is not None:
                s = jnp.where(mask, s, NEG_BIG)
            s_ref[slot, hd] = s
            m_old = m_prev(hd)
            m_new = jnp.maximum(m_old, jnp.max(s, axis=0, keepdims=True))
            m_ref[slot, hd, 0:1, :] = m_new
            al_ref[lax.rem(blk, 3), hd, 0:1, :] = jnp.exp2(m_old - m_new)

    def values(blk, slot):
        return [_dot(vt_ref[0, hd, blk], p_ref[slot, hd]) for hd in range(g)]

    def accumulate(alphas, pvs):
        for hd in range(g):
            acc_ref[hd] = alphas[hd] * acc_ref[hd] + pvs[hd]

    def iteration(j, cur, next_mask, has_next):
        alphas = [al_ref[lax.rem(j + 2, 3), hd, 0:1, :] for hd in range(g)]
        pvs = values(jnp.maximum(j - 1, 0), 1 - cur)
        for hd in range(g):
            p_ref[cur, hd] = jnp.exp2(s_ref[cur, hd] - m_ref[cur, hd, 0:1, :]).astype(BF16)
        if has_next:
            scores(j + 1, 1 - cur, next_mask, lambda hd: m_ref[cur, hd, 0:1, :])
        accumulate(alphas, pvs)

    p_ref[1] = jnp.zeros(p_ref.shape[1:], BF16)
    al_ref[2] = jnp.ones(al_ref.shape[1:], F32)
    acc_ref[...] = jnp.zeros_like(acc_ref)
    scores(0, 0, key <= qry + jnp.where(i > 0, t, 0), lambda hd: jnp.full((1, t), NEG_BIG, F32))

    def pair(jj, carry):
        iteration(2 * jj, 0, None, True)
        iteration(2 * jj + 1, 1, None, True)
        return carry

    lax.fori_loop(0, jnp.maximum(i - 1, 0) >> 1, pair, 0)

    @pl.when((i >= 2) & ((i & 1) == 0))
    def _():
        iteration(i - 2, 0, None, True)

    @pl.when(i >= 1)
    def _():
        iteration(i - 1, (i - 1) & 1, key <= qry, True)

    iteration(i, i & 1, None, False)
    accumulate([al_ref[lax.rem(i, 3), hd, 0:1, :] for hd in range(g)], values(i, i & 1))
    for hd in range(g):
        acc = acc_ref[hd]
        o_ref[0, hd * HEAD_DIM:(hd + 1) * HEAD_DIM, :] = acc[0:HEAD_DIM] / acc[HEAD_DIM:HEAD_DIM + 1]


def _attn_prompt(qt, ka, vt, g):
    b, h, _, s = qt.shape
    t = vt.shape[-1]
    nq = s // t
    return pl.pallas_call(
        _attn_prompt_kernel,
        out_shape=jax.ShapeDtypeStruct((b, WIDTH_A, s), F32),
        grid=(b, h // g, nq),
        in_specs=[
            pl.BlockSpec((1, g, LANES, t), lambda bi, hi, i: (bi, hi, 0, i)),
            pl.BlockSpec((1, g, s, LANES), lambda bi, hi, i: (bi, hi, 0, 0)),
            pl.BlockSpec((1, g, nq, VT_ROWS, t), lambda bi, hi, i: (bi, hi, 0, 0, 0)),
        ],
        out_specs=pl.BlockSpec((1, g * HEAD_DIM, t), lambda bi, hi, i: (bi, hi, i)),
        scratch_shapes=[pltpu.VMEM((2, g, t, t), F32), pltpu.VMEM((2, g, t, t), BF16),
                        pltpu.VMEM((2, g, 8, t), F32), pltpu.VMEM((3, g, 8, t), F32),
                        pltpu.VMEM((g, VT_ROWS, t), F32)],
        compiler_params=pltpu.CompilerParams(
            dimension_semantics=("arbitrary", "arbitrary", "arbitrary"), vmem_limit_bytes=VMEM_LIMIT),
        name="attn_prompt",
    )(qt, ka, vt)


def _attn_sample_kernel(n_pages, t_new, pt_ref, q_ref, kn_ref, vn_ref, ln_ref, ls_ref, ones_ref, m3_ref,
                        eh_ref, ehh_ref, *rest):
    del pt_ref
    k_refs = rest[:n_pages]
    v_refs = rest[n_pages:2 * n_pages]
    l_refs = rest[2 * n_pages:3 * n_pages]
    o_ref = rest[3 * n_pages]
    rows = N_HEADS * t_new

    q = q_ref[...]
    row = lax.broadcasted_iota(jnp.int32, (rows, WIDTH_A), 0)
    lane = lax.broadcasted_iota(jnp.int32, (rows, WIDTH_A), 1)
    qbd = jnp.where((row >> 3) == (lane >> 6), jnp.concatenate([q] * N_HEADS, axis=0), 0.0).astype(BF16)

    cn = ln_ref[...] * LOG2E
    trow = lax.broadcasted_iota(jnp.int32, (t_new, LANES), 0)
    sh = 1
    while sh < t_new:
        cn = cn + jnp.where(trow >= sh, pltpu.roll(cn, sh, axis=0), 0.0)
        sh *= 2
    row1 = lax.broadcasted_iota(jnp.int32, (rows, LANES), 0)
    lane1 = lax.broadcasted_iota(jnp.int32, (rows, LANES), 1)
    cn_rows = jnp.concatenate([cn] * N_HEADS, axis=0)
    cn_col = jnp.sum(jnp.where((row1 >> 3) == lane1, cn_rows, 0.0), axis=1, keepdims=True)

    x = jnp.concatenate([l_refs[p][0, 0] for p in range(n_pages)], axis=0) * LOG2E
    xp = _split3(x)
    r_in = _dot3_right(xp, ls_ref[...])
    tot_b = _dot3_right(xp, ones_ref[...])
    r_page = _dot3_left(m3_ref[...], _split3(tot_b))
    rp = _split3(r_in + r_page)
    bias_past = _dot3_left(eh_ref[...], rp)

    s_blocks = []
    for p in range(n_pages):
        kp_t = k_refs[p][0, 0].astype(BF16)
        s_blocks.append(_dot(qbd, kp_t) + bias_past[p * rows:(p + 1) * rows] + cn_col)
    pad = jnp.zeros((PAGE - t_new, WIDTH_A), F32)
    kn = jnp.concatenate([kn_ref[...], pad], axis=0).astype(BF16)
    vn = jnp.concatenate([vn_ref[...], pad], axis=0).astype(BF16)
    cn_pad = jnp.concatenate([cn, jnp.zeros((PAGE - t_new, LANES), F32)], axis=0)
    cnp = _split3(cn_pad)
    ehh = ehh_ref[...]
    cn_keys = _dot_nt(ehh, cnp[0]) + _dot_nt(ehh, cnp[1]) + _dot_nt(ehh, cnp[2])
    s_new = _dot_nt(qbd, kn) + cn_col - cn_keys
    s_new = jnp.where((lane1 <= (row1 & (t_new - 1))) & (lane1 < t_new), s_new, NEG_BIG)
    s_blocks.append(s_new)

    m = s_blocks[0]
    for sb in s_blocks[1:]:
        m = jnp.maximum(m, sb)
    m = jnp.max(m, axis=1, keepdims=True)
    l_sum = jnp.zeros((rows, 1), F32)
    o = jnp.zeros((rows, WIDTH_A), F32)
    for p in range(n_pages + 1):
        pe = jnp.exp2(s_blocks[p] - m)
        l_sum = l_sum + jnp.sum(pe, axis=1, keepdims=True)
        if p < n_pages:
            o = o + _dot_nt(pe.astype(BF16), v_refs[p][0, 0].astype(BF16))
        else:
            o = o + _dot(pe.astype(BF16), vn)
    o = o / l_sum
    lane_h = lax.broadcasted_iota(jnp.int32, (t_new, WIDTH_A), 1) >> 6
    out = jnp.zeros((t_new, WIDTH_A), F32)
    for hd in range(N_HEADS):
        out = out + jnp.where(lane_h == hd, o[hd * t_new:(hd + 1) * t_new], 0.0)
    o_ref[...] = out


def _attn_sample(layer, page_table, q, kn, vn, lnp, ck, cv, clt, t_new):
    n_seq, n_pages = page_table.shape
    rows = N_HEADS * t_new
    pr = n_pages * N_HEADS
    key = np.arange(PAGE)
    ls = (key[:, None] > key[None, :]).astype(np.float32)
    ones = np.ones((PAGE, LANES), np.float32)
    ph = np.arange(pr)
    m3 = ((ph[:, None] % N_HEADS == ph[None, :] % N_HEADS)
          & (ph[None, :] // N_HEADS > ph[:, None] // N_HEADS)).astype(np.float32)
    r_all = np.arange(n_pages * rows)
    eh = ((r_all[:, None] // rows == ph[None, :] // N_HEADS)
          & ((r_all[:, None] % rows) // t_new == ph[None, :] % N_HEADS)).astype(np.float32)
    ehh = ((np.arange(rows)[:, None] // t_new) == np.arange(LANES)[None, :]).astype(np.float32)
    consts = tuple(jnp.asarray(a, BF16) for a in (ls, ones, m3, eh, ehh))

    tok = lambda width: pl.BlockSpec((t_new, width), lambda b, pt: (b, 0))
    def page_specs(block):
        return [pl.BlockSpec(block, functools.partial(lambda b, pt, p: (layer, pt[b, p], 0, 0), p=p))
                for p in range(n_pages)]
    grid_spec = pltpu.PrefetchScalarGridSpec(
        num_scalar_prefetch=1,
        grid=(n_seq,),
        in_specs=[tok(WIDTH_A), tok(WIDTH_A), tok(WIDTH_A), tok(LANES)]
        + [pl.BlockSpec(c.shape, lambda b, pt: (0, 0)) for c in consts]
        + page_specs((1, 1, WIDTH_A, PAGE)) + page_specs((1, 1, WIDTH_A, PAGE))
        + page_specs((1, 1, N_HEADS, PAGE)),
        out_specs=tok(WIDTH_A),
    )
    return pl.pallas_call(
        functools.partial(_attn_sample_kernel, n_pages, t_new),
        out_shape=jax.ShapeDtypeStruct((n_seq * t_new, WIDTH_A), F32),
        grid_spec=grid_spec,
        compiler_params=pltpu.CompilerParams(
            dimension_semantics=("arbitrary",), vmem_limit_bytes=VMEM_LIMIT),
        name="attn_sample",
    )(page_table, q, kn, vn, lnp, *consts, *([ck] * n_pages), *([cv] * n_pages), *([clt] * n_pages))


def _merge_kernel(transposed, a_ref, ga_ref, bm_ref, sa_ref, sb_ref, x_ref, wpa_ref, wpb_ref, wo_ref, y_ref):
    tm = x_ref.shape[-2]
    attn = a_ref[0].T if transposed else a_ref[...]
    ga = ga_ref[...].reshape(tm, WIDTH_A)
    a = (attn * ga.astype(F32)).astype(BF16)
    pa = _dot(a, wpa_ref[...])
    pb = _dot(bm_ref[...].reshape(tm, WIDTH_B), wpb_ref[...])
    m = sa_ref[...].reshape(tm, D_MODEL).astype(F32) * pa + sb_ref[...].reshape(tm, D_MODEL).astype(F32) * pb
    y = x_ref[...].reshape(tm, D_MODEL) + _dot(m.astype(BF16), wo_ref[...])
    y_ref[...] = y.reshape(y_ref.shape)


def _merge_prompt(attn_t, ga, bm, sa, sb, x, wpa, wpb, wo, tm):
    b, s, _ = x.shape
    row3 = lambda width: pl.BlockSpec((1, tm, width), lambda bi, i: (bi, i, 0))
    return pl.pallas_call(
        functools.partial(_merge_kernel, True),
        out_shape=jax.ShapeDtypeStruct((b, s, D_MODEL), F32),
        grid=(b, s // tm),
        in_specs=[pl.BlockSpec((1, WIDTH_A, tm), lambda bi, i: (bi, 0, i)),
                  row3(WIDTH_A), row3(WIDTH_B), row3(D_MODEL), row3(D_MODEL), row3(D_MODEL),
                  _const_spec(wpa.shape), _const_spec(wpb.shape), _const_spec(wo.shape)],
        out_specs=row3(D_MODEL),
        compiler_params=pltpu.CompilerParams(
            dimension_semantics=("arbitrary", "arbitrary"), vmem_limit_bytes=VMEM_LIMIT),
        name="merge_prompt",
    )(attn_t, ga, bm, sa, sb, x, wpa, wpb, wo)


def _merge_sample(attn, ga, bm, sa, sb, x, wpa, wpb, wo, tm):
    n = x.shape[0]
    row2 = lambda width: pl.BlockSpec((tm, width), lambda i: (i, 0))
    return pl.pallas_call(
        functools.partial(_merge_kernel, False),
        out_shape=jax.ShapeDtypeStruct((n, D_MODEL), F32),
        grid=(n // tm,),
        in_specs=[row2(WIDTH_A), row2(WIDTH_A), row2(WIDTH_B), row2(D_MODEL), row2(D_MODEL), row2(D_MODEL),
                  _const_spec(wpa.shape), _const_spec(wpb.shape), _const_spec(wo.shape)],
        out_specs=row2(D_MODEL),
        compiler_params=pltpu.CompilerParams(
            dimension_semantics=("arbitrary",), vmem_limit_bytes=VMEM_LIMIT),
        name="merge_sample",
    )(attn, ga, bm, sa, sb, x, wpa, wpb, wo)


TM_PROMPT = 256
HEADS_PER_STEP = 4
TM_MERGE = 512
TM_SAMPLE = 256


def kernel(x_prompt, x_sample, cache_k, cache_v, cache_logf, page_table, norm_g, w_in, b_forget, q_norm_g,
           k_norm_g, v_norm_g, w_s, b_s, w_proj_a, w_proj_b, w_out):
    depth = w_in.shape[0]
    bsz, seq, _ = x_prompt.shape
    n_seq, t_new, _ = x_sample.shape
    n_pool = cache_k.shape[1]

    f_lo = 4 * WIDTH_A
    w_perm = jnp.concatenate(
        [w_in[:, :, :f_lo], w_in[:, :, f_lo + N_HEADS:], w_in[:, :, f_lo:f_lo + N_HEADS],
         jnp.zeros((depth, D_MODEL, C_END - C_F - N_HEADS), w_in.dtype)], axis=-1).astype(BF16)
    bf_pad = jnp.pad(b_forget, ((0, 0), (0, LANES - N_HEADS)))[:, None, :]
    qg = jnp.tile(q_norm_g, (1, N_HEADS))[:, None, :]
    kg = jnp.tile(k_norm_g, (1, N_HEADS))[:, None, :]
    vg = v_norm_g[:, None, :]
    ng = norm_g[:, None, :]
    wcat_p = w_s.transpose(0, 2, 1, 3).reshape(depth, CHUNK, N_GROUPS * CHUNK).astype(BF16)
    w_small = jnp.tile(w_s[:, :, :t_new, :t_new], (1, 1, CHUNK // t_new, CHUNK // t_new))
    wcat_s = w_small.transpose(0, 2, 1, 3).reshape(depth, CHUNK, N_GROUPS * CHUNK).astype(BF16)
    sbias_p = jnp.repeat(b_s.transpose(0, 2, 1), GROUP_DIM, axis=-1)
    sbias_s = jnp.repeat(jnp.tile(b_s[:, :, :t_new], (1, 1, CHUNK // t_new)).transpose(0, 2, 1),
                         GROUP_DIM, axis=-1)
    wpa = w_proj_a.astype(BF16)
    wpb = w_proj_b.astype(BF16)
    wo = w_out.astype(BF16)
    ck = cache_k.transpose(0, 1, 3, 4, 2).reshape(depth, n_pool, WIDTH_A, PAGE)
    cv = cache_v.transpose(0, 1, 3, 4, 2).reshape(depth, n_pool, WIDTH_A, PAGE)
    clt = cache_logf.transpose(0, 1, 3, 2)

    xp = x_prompt
    xs = x_sample.reshape(n_seq * t_new, D_MODEL)
    outs = [[] for _ in range(7)]
    for l in range(depth):
        common = (ng[l], w_perm[l], bf_pad[l], qg[l], kg[l], vg[l])
        qt, ka, vt, k_new, v_new, logf, ga, bm, sa, sb = _in_proj_prompt(
            xp, common + (wcat_p[l], sbias_p[l]), TM_PROMPT)
        attn_t = _attn_prompt(qt, ka, vt, HEADS_PER_STEP)
        xp = _merge_prompt(attn_t, ga, bm, sa, sb, xp, wpa[l], wpb[l], wo[l], TM_MERGE)
        outs[0].append(k_new.reshape(bsz, seq, N_HEADS, HEAD_DIM))
        outs[1].append(v_new.reshape(bsz, seq, N_HEADS, HEAD_DIM))
        outs[2].append(logf)

        q_s, k_s, v_s, lf_s, lfp_s, vbn_s, ga, bm, sa, sb = _in_proj_sample(
            xs, common + (wcat_s[l], sbias_s[l]), TM_SAMPLE, t_new)
        attn_s = _attn_sample(l, page_table, q_s, k_s, v_s, lfp_s, ck, cv, clt, t_new)
        xs = _merge_sample(attn_s, ga, bm, sa, sb, xs, wpa[l], wpb[l], wo[l], TM_SAMPLE)
        outs[3].append(k_s.reshape(n_seq, t_new, N_HEADS, HEAD_DIM))
        outs[4].append(v_s.reshape(n_seq, t_new, N_HEADS, HEAD_DIM))
        outs[5].append(lf_s.reshape(n_seq, t_new, N_HEADS))
        outs[6].append(vbn_s.reshape(n_seq, t_new, WIDTH_B))
    return (xp, xs.reshape(n_seq, t_new, D_MODEL)) + tuple(jnp.stack(o) for o in outs)
```

```python
import functools
import math

import numpy as np
import jax
import jax.numpy as jnp
from jax import lax
from jax.experimental import pallas as pl
from jax.experimental.pallas import tpu as pltpu

D_MODEL = 1024
N_HEADS = 8
HEAD_DIM = 64
WIDTH_A = N_HEADS * HEAD_DIM
N_GROUPS = 8
GROUP_DIM = 64
WIDTH_B = N_GROUPS * GROUP_DIM
CHUNK = 128
PAGE = 128
EPS = 1e-6
LOG2E = math.log2(math.e)
NEG_BIG = -1e30

LANES = 128
VT_ROWS = 80
VMEM_LIMIT = 56 * 1024 * 1024

C_Q, C_K, C_V, C_GA, C_U, C_VB, C_GB, C_SA, C_SB, C_F, C_END = (
    0, 512, 1024, 1536, 2048, 2560, 3072, 3584, 4608, 5632, 5760)

F32 = jnp.float32
BF16 = jnp.bfloat16


def _split3(x):
    hi = x.astype(BF16)
    r1 = x - hi.astype(F32)
    mid = r1.astype(BF16)
    lo = (r1 - mid.astype(F32)).astype(BF16)
    return hi, mid, lo


def _dot(a, b):
    return jnp.dot(a, b, preferred_element_type=F32)


def _dot_nt(a, b):
    return lax.dot_general(a, b, (((1,), (1,)), ((), ())), preferred_element_type=F32)


def _dot3_right(parts, w):
    return _dot(parts[0], w) + _dot(parts[1], w) + _dot(parts[2], w)


def _dot3_left(w, parts):
    return _dot(w, parts[0]) + _dot(w, parts[1]) + _dot(w, parts[2])


def _sigmoid(x):
    return 1.0 / (1.0 + jnp.exp(-x))


def _gelu_tanh(x):
    c = math.sqrt(2.0 / math.pi)
    return 0.5 * x * (1.0 + jnp.tanh(c * (x + 0.044715 * (x * x * x))))


def _log_sigmoid(x):
    return jnp.minimum(x, 0.0) - jnp.log1p(jnp.exp(-jnp.abs(x)))


def _in_proj_common(x_ref, g_ref, w_ref, bf_ref, qg_ref, kg_ref, vg_ref, bd_ref, wcat_ref, sbias_ref,
                    seq_local):
    tm = x_ref.shape[-2]
    x = x_ref[...].reshape(tm, D_MODEL)
    ms = jnp.mean(x * x, axis=-1, keepdims=True)
    h = ((x * lax.rsqrt(ms + EPS)) * g_ref[...]).astype(BF16)

    def seg(lo, hi):
        return _dot(h, w_ref[:, lo:hi])

    bd = bd_ref[...]

    def head_norm(z, gain):
        msz = _dot((z * z).astype(BF16), bd) * (1.0 / HEAD_DIM)
        return z * lax.rsqrt(msz + EPS) * gain

    qn = head_norm(seg(C_Q, C_K), qg_ref[...])
    kn = head_norm(seg(C_K, C_V), kg_ref[...])
    v = seg(C_V, C_GA)
    zga = seg(C_GA, C_U)
    ga = zga * _sigmoid(zga)

    lane = lax.broadcasted_iota(jnp.int32, (tm, LANES), 1)
    logf = jnp.where(lane < N_HEADS, _log_sigmoid(seg(C_F, C_END) + bf_ref[...]), 0.0)

    u = _gelu_tanh(seg(C_U, C_VB))
    vbg = _gelu_tanh(seg(C_VB, C_GB))
    msv = jnp.mean(vbg * vbg, axis=-1, keepdims=True)
    vbn = vbg * lax.rsqrt(msv + EPS) * vg_ref[...]
    zgb = seg(C_GB, C_SA)
    ugb = u * (zgb * _sigmoid(zgb))

    t_idx = lax.broadcasted_iota(jnp.int32, (CHUNK, N_GROUPS * CHUNK), 0)
    s_idx = lax.broadcasted_iota(jnp.int32, (CHUNK, N_GROUPS * CHUNK), 1) & (CHUNK - 1)
    keep = s_idx <= t_idx
    if seq_local < CHUNK:
        shift = int(math.log2(seq_local))
        keep = keep & ((s_idx >> shift) == (t_idx >> shift))
    wm = jnp.where(keep, wcat_ref[...], jnp.zeros((), BF16))
    glane = lax.broadcasted_iota(jnp.int32, (CHUNK, WIDTH_B), 1) >> 6
    vbn_b = vbn.astype(BF16)
    s_chunks = []
    for c in range(tm // CHUNK):
        vc = vbn_b[c * CHUNK:(c + 1) * CHUNK]
        rhs = jnp.concatenate(
            [jnp.where(glane == g, vc, jnp.zeros((), BF16)) for g in range(N_GROUPS)], axis=0)
        s_chunks.append(_dot(wm, rhs) + sbias_ref[...])
    s_sgu = jnp.concatenate(s_chunks, axis=0)
    bmix = ugb * s_sgu

    sa = _sigmoid(seg(C_SA, C_SB))
    sb = _sigmoid(seg(C_SB, C_F))
    return qn, kn, v, logf, vbn, ga, bmix, sa, sb


def _in_proj_prompt_kernel(n_alias, *refs):
    (x_ref, g_ref, w_ref, bf_ref, qg_ref, kg_ref, vg_ref, bd_ref, wcat_ref, sbias_ref, tri_ref, pk_ref,
     pq_ref, qt_ref, ka_ref, vt_ref, ko_ref, vo_ref, lf_ref, ga_ref, bm_ref, sa_ref, sb_ref,
     carry_ref) = refs[n_alias:]
    tm = x_ref.shape[-2]
    qn, kn, v, logf, _, ga, bmix, sa, sb = _in_proj_common(
        x_ref, g_ref, w_ref, bf_ref, qg_ref, kg_ref, vg_ref, bd_ref, wcat_ref, sbias_ref, CHUNK)

    lf_ref[0, 0] = logf.T[0:N_HEADS]
    ga_ref[0] = ga.astype(BF16)
    bm_ref[0] = bmix.astype(BF16)
    sa_ref[0] = sa.astype(BF16)
    sb_ref[0] = sb.astype(BF16)

    @pl.when(pl.program_id(1) == 0)
    def _():
        carry_ref[...] = jnp.zeros_like(carry_ref)

    c = _dot3_left(tri_ref[...], _split3(logf)) + carry_ref[0:1, :]
    carry_ref[0:1, :] = c[tm - 1:tm, :]
    c2 = c * LOG2E

    lane = lax.broadcasted_iota(jnp.int32, (tm, LANES), 1)
    hi, mid, lo = _split3(c2)
    cp = (hi.astype(F32) + pltpu.roll(mid.astype(F32), 8, axis=1)
          + pltpu.roll(lo.astype(F32), 16, axis=1) + jnp.where(lane == 24, 1.0, 0.0))
    extra_k = _dot(cp.astype(BF16), pk_ref[...])

    c2t = c2.T
    hit, midt, lot = (part.astype(F32) for part in _split3(c2t[0:8]))
    row = lax.broadcasted_iota(jnp.int32, (8, tm), 0)
    cpt = jnp.concatenate(
        [hit, midt, lot, jnp.where(row == 0, 1.0, 0.0), jnp.zeros((LANES - 32, tm), F32)],
        axis=0).astype(BF16)
    extra_q = _dot(pq_ref[...], cpt)

    q2 = qn * (HEAD_DIM ** -0.5 * LOG2E)
    ones_blk = jnp.where(lax.broadcasted_iota(jnp.int32, (VT_ROWS - HEAD_DIM, tm), 0) == 0, 1.0, 0.0)
    for j in range(N_HEADS // 2):
        qs_t = q2[:, j * LANES:(j + 1) * LANES].T
        vs_t = v[:, j * LANES:(j + 1) * LANES].T
        ks = kn[:, j * LANES:(j + 1) * LANES]
        ks_sw = pltpu.roll(ks, HEAD_DIM, axis=1)
        ko_ref[0, 0, j * LANES:(j + 1) * LANES, :] = ks.T
        vo_ref[0, 0, j * LANES:(j + 1) * LANES, :] = vs_t
        for hh in range(2):
            hd = 2 * j + hh
            qt_ref[0, hd] = jnp.concatenate(
                [qs_t[hh * HEAD_DIM:(hh + 1) * HEAD_DIM], extra_q[hd * HEAD_DIM:(hd + 1) * HEAD_DIM]],
                axis=0).astype(BF16)
            vt_ref[0, hd, 0] = jnp.concatenate(
                [vs_t[hh * HEAD_DIM:(hh + 1) * HEAD_DIM], ones_blk], axis=0).astype(BF16)
            ka_ref[0, hd] = jnp.where(lane < HEAD_DIM, ks if hh == 0 else ks_sw,
                                      extra_k[:, hd * LANES:(hd + 1) * LANES]).astype(BF16)


def _in_proj_sample_kernel(x_ref, g_ref, w_ref, bf_ref, qg_ref, kg_ref, vg_ref, bd_ref, wcat_ref,
                           sbias_ref,
                           q_ref, ko_ref, vo_ref, lf_ref, lfp_ref, vbn_ref, ga_ref, bm_ref, sa_ref, sb_ref,
                           *, seq_local):
    qn, kn, v, logf, vbn, ga, bmix, sa, sb = _in_proj_common(
        x_ref, g_ref, w_ref, bf_ref, qg_ref, kg_ref, vg_ref, bd_ref, wcat_ref, sbias_ref, seq_local)
    q_ref[...] = qn * (HEAD_DIM ** -0.5 * LOG2E)
    ko_ref[...] = kn
    vo_ref[...] = v
    lf_ref[...] = logf[:, :N_HEADS]
    lfp_ref[...] = logf
    vbn_ref[...] = vbn
    ga_ref[...] = ga.astype(BF16)
    bm_ref[...] = bmix.astype(BF16)
    sa_ref[...] = sa.astype(BF16)
    sb_ref[...] = sb.astype(BF16)


def _const_spec(shape):
    return pl.BlockSpec(shape, lambda *_: (0,) * len(shape))


def _in_proj_consts(tm):
    bd = np.kron(np.eye(N_HEADS, dtype=np.float32), np.ones((HEAD_DIM, HEAD_DIM), np.float32))
    tri = np.tril(np.ones((tm, tm), np.float32))
    pk = np.zeros((LANES, N_HEADS * LANES), np.float32)
    pq = np.zeros((N_HEADS * HEAD_DIM, LANES), np.float32)
    for hd in range(N_HEADS):
        for part in range(3):
            pk[24, hd * LANES + HEAD_DIM + part] = 1.0
            pk[8 * part + hd, hd * LANES + HEAD_DIM + 3 + part] = -1.0
            pq[hd * HEAD_DIM + part, 8 * part + hd] = 1.0
            pq[hd * HEAD_DIM + 3 + part, 24] = 1.0
    return (jnp.asarray(bd, BF16), jnp.asarray(tri, BF16), jnp.asarray(pk, BF16), jnp.asarray(pq, BF16))


def _in_proj_prompt(x, params, tm, layer, depth, stacked):
    b, s, _ = x.shape
    nt = s // tm
    bd, tri, pk, pq = _in_proj_consts(tm)
    g, w, bf, qg, kg, vg, wcat, sbias = params
    consts = (g, w, bf, qg, kg, vg, bd, wcat, sbias, tri, pk, pq)
    stacked = () if stacked is None else tuple(stacked)
    row3 = lambda width: pl.BlockSpec((1, tm, width), lambda bi, i: (bi, i, 0))
    col4 = lambda rows: pl.BlockSpec((1, 1, rows, tm), lambda bi, i: (layer, bi, 0, i))
    out_shape = (
        jax.ShapeDtypeStruct((b, N_HEADS, LANES, s), BF16),
        jax.ShapeDtypeStruct((b, N_HEADS, s, LANES), BF16),
        jax.ShapeDtypeStruct((b, N_HEADS, nt, VT_ROWS, tm), BF16),
        jax.ShapeDtypeStruct((depth, b, WIDTH_A, s), F32),
        jax.ShapeDtypeStruct((depth, b, WIDTH_A, s), F32),
        jax.ShapeDtypeStruct((depth, b, N_HEADS, s), F32),
        jax.ShapeDtypeStruct((b, s, WIDTH_A), BF16),
        jax.ShapeDtypeStruct((b, s, WIDTH_B), BF16),
        jax.ShapeDtypeStruct((b, s, D_MODEL), BF16),
        jax.ShapeDtypeStruct((b, s, D_MODEL), BF16),
    )
    out_specs = (
        pl.BlockSpec((1, N_HEADS, LANES, tm), lambda bi, i: (bi, 0, 0, i)),
        pl.BlockSpec((1, N_HEADS, tm, LANES), lambda bi, i: (bi, 0, i, 0)),
        pl.BlockSpec((1, N_HEADS, 1, VT_ROWS, tm), lambda bi, i: (bi, 0, i, 0, 0)),
        col4(WIDTH_A), col4(WIDTH_A), col4(N_HEADS), row3(WIDTH_A), row3(WIDTH_B),
        row3(D_MODEL), row3(D_MODEL),
    )
    return pl.pallas_call(
        functools.partial(_in_proj_prompt_kernel, len(stacked)),
        out_shape=out_shape,
        grid=(b, nt),
        in_specs=[pl.BlockSpec(memory_space=pl.ANY)] * len(stacked) + [row3(D_MODEL)]
        + [_const_spec(c.shape) for c in consts],
        out_specs=out_specs,
        scratch_shapes=[pltpu.VMEM((8, LANES), F32)],
        input_output_aliases={n: 3 + n for n in range(len(stacked))},
        compiler_params=pltpu.CompilerParams(
            dimension_semantics=("arbitrary", "arbitrary"), vmem_limit_bytes=VMEM_LIMIT),
        name="in_proj_prompt",
    )(*stacked, x, *consts)


def _in_proj_sample(x, params, tm, seq_local):
    n = x.shape[0]
    bd = _in_proj_consts(tm)[0]
    g, w, bf, qg, kg, vg, wcat, sbias = params
    consts = (g, w, bf, qg, kg, vg, bd, wcat, sbias)
    row2 = lambda width: pl.BlockSpec((tm, width), lambda i: (i, 0))
    out_shape = (
        jax.ShapeDtypeStruct((n, WIDTH_A), F32),
        jax.ShapeDtypeStruct((n, WIDTH_A), F32),
        jax.ShapeDtypeStruct((n, WIDTH_A), F32),
        jax.ShapeDtypeStruct((n, N_HEADS), F32),
        jax.ShapeDtypeStruct((n, LANES), F32),
        jax.ShapeDtypeStruct((n, WIDTH_B), F32),
        jax.ShapeDtypeStruct((n, WIDTH_A), BF16),
        jax.ShapeDtypeStruct((n, WIDTH_B), BF16),
        jax.ShapeDtypeStruct((n, D_MODEL), BF16),
        jax.ShapeDtypeStruct((n, D_MODEL), BF16),
    )
    out_specs = (row2(WIDTH_A), row2(WIDTH_A), row2(WIDTH_A), row2(N_HEADS), row2(LANES), row2(WIDTH_B),
                 row2(WIDTH_A), row2(WIDTH_B), row2(D_MODEL), row2(D_MODEL))
    return pl.pallas_call(
        functools.partial(_in_proj_sample_kernel, seq_local=seq_local),
        out_shape=out_shape,
        grid=(n // tm,),
        in_specs=[row2(D_MODEL)] + [_const_spec(c.shape) for c in consts],
        out_specs=out_specs,
        compiler_params=pltpu.CompilerParams(
            dimension_semantics=("arbitrary",), vmem_limit_bytes=VMEM_LIMIT),
        name="in_proj_sample",
    )(x, *consts)


def _attn_prompt_kernel(qt_ref, ka_ref, vt_ref, o_ref, s_ref, p_ref, m_ref, al_ref, acc_ref):
    g = qt_ref.shape[1]
    tq = qt_ref.shape[-1]
    t = vt_ref.shape[-1]
    assert tq == 2 * t
    i = pl.program_id(2)
    key = lax.broadcasted_iota(jnp.int32, (t, tq), 0)
    qry = lax.broadcasted_iota(jnp.int32, (t, tq), 1)

    def scores(hd, blk, slot, mask, m_old):
        k = ka_ref[0, hd, pl.ds(pl.multiple_of(blk * t, t), t), :]
        s = _dot(k, qt_ref[0, hd])
        if mask is not None:
            s = jnp.where(mask, s, NEG_BIG)
        s_ref[slot, hd] = s
        m_new = jnp.maximum(m_old, jnp.max(s, axis=0, keepdims=True))
        m_ref[slot, hd, 0:1, :] = m_new
        al_ref[lax.rem(blk, 3), hd, 0:1, :] = jnp.exp2(m_old - m_new)

    def values(hd, blk, slot):
        return _dot(vt_ref[0, hd, blk], p_ref[slot, hd])

    def accumulate(hd, alpha, pv):
        acc_ref[hd] = alpha * acc_ref[hd] + pv

    def iteration(j, cur, next_mask, has_next):
        alphas = [al_ref[lax.rem(j + 2, 3), hd, 0:1, :] for hd in range(g)]
        for hd in range(g):
            p_ref[cur, hd] = jnp.exp2(s_ref[cur, hd] - m_ref[cur, hd, 0:1, :]).astype(BF16)
        for hd in range(g):
            pv = values(hd, jnp.maximum(j - 1, 0), 1 - cur)
            if has_next:
                scores(hd, j + 1, 1 - cur, next_mask, m_ref[cur, hd, 0:1, :])
            accumulate(hd, alphas[hd], pv)

    p_ref[1] = jnp.zeros(p_ref.shape[1:], BF16)
    al_ref[2] = jnp.ones(al_ref.shape[1:], F32)
    acc_ref[...] = jnp.zeros_like(acc_ref)
    diag_lo = key <= qry
    diag_hi = key + t <= qry
    for hd in range(g):
        scores(hd, 0, 0, key <= qry + jnp.where(i > 0, tq, 0), jnp.full((1, tq), NEG_BIG, F32))

    def pair(jj, carry):
        iteration(2 * jj, 0, None, True)
        iteration(2 * jj + 1, 1, None, True)
        return carry

    lax.fori_loop(0, i - 1, pair, 0)

    @pl.when(i >= 1)
    def _():
        iteration(2 * i - 2, 0, None, True)
        iteration(2 * i - 1, 1, diag_lo, True)

    last = 2 * i + 1
    iteration(last - 1, 0, diag_hi, True)
    iteration(last, 1, None, False)
    for hd in range(g):
        accumulate(hd, al_ref[lax.rem(last, 3), hd, 0:1, :], values(hd, last, 1))
    for hd in range(g):
        acc = acc_ref[hd]
        o_ref[0, hd * HEAD_DIM:(hd + 1) * HEAD_DIM, :] = acc[0:HEAD_DIM] / acc[HEAD_DIM:HEAD_DIM + 1]


def _attn_prompt(qt, ka, vt, g):
    b, h, _, s = qt.shape
    t = vt.shape[-1]
    tq = 2 * t
    return pl.pallas_call(
        _attn_prompt_kernel,
        out_shape=jax.ShapeDtypeStruct((b, WIDTH_A, s), F32),
        grid=(b, h // g, s // tq),
        in_specs=[
            pl.BlockSpec((1, g, LANES, tq), lambda bi, hi, i: (bi, hi, 0, i)),
            pl.BlockSpec((1, g, s, LANES), lambda bi, hi, i: (bi, hi, 0, 0),
                         pipeline_mode=pl.Buffered(1)),
            pl.BlockSpec((1, g, s // t, VT_ROWS, t), lambda bi, hi, i: (bi, hi, 0, 0, 0),
                         pipeline_mode=pl.Buffered(1)),
        ],
        out_specs=pl.BlockSpec((1, g * HEAD_DIM, tq), lambda bi, hi, i: (bi, hi, i)),
        scratch_shapes=[pltpu.VMEM((2, g, t, tq), F32), pltpu.VMEM((2, g, t, tq), BF16),
                        pltpu.VMEM((2, g, 8, tq), F32), pltpu.VMEM((3, g, 8, tq), F32),
                        pltpu.VMEM((g, VT_ROWS, tq), F32)],
        compiler_params=pltpu.CompilerParams(
            dimension_semantics=("arbitrary", "arbitrary", "arbitrary"), vmem_limit_bytes=VMEM_LIMIT),
        name="attn_prompt",
    )(qt, ka, vt)


def _attn_sample_kernel(n_pages, t_new, pt_ref, q_ref, kn_ref, vn_ref, ln_ref, ls_ref, ones_ref, m3_ref,
                        eh_ref, ehh_ref, *rest):
    del pt_ref
    k_refs = rest[:n_pages]
    v_refs = rest[n_pages:2 * n_pages]
    l_refs = rest[2 * n_pages:3 * n_pages]
    o_ref = rest[3 * n_pages]
    rows = N_HEADS * t_new

    q = q_ref[...]
    row = lax.broadcasted_iota(jnp.int32, (rows, WIDTH_A), 0)
    lane = lax.broadcasted_iota(jnp.int32, (rows, WIDTH_A), 1)
    qbd = jnp.where((row >> 3) == (lane >> 6), jnp.concatenate([q] * N_HEADS, axis=0), 0.0).astype(BF16)

    cn = ln_ref[...] * LOG2E
    trow = lax.broadcasted_iota(jnp.int32, (t_new, LANES), 0)
    sh = 1
    while sh < t_new:
        cn = cn + jnp.where(trow >= sh, pltpu.roll(cn, sh, axis=0), 0.0)
        sh *= 2
    row1 = lax.broadcasted_iota(jnp.int32, (rows, LANES), 0)
    lane1 = lax.broadcasted_iota(jnp.int32, (rows, LANES), 1)
    cn_rows = jnp.concatenate([cn] * N_HEADS, axis=0)
    cn_col = jnp.sum(jnp.where((row1 >> 3) == lane1, cn_rows, 0.0), axis=1, keepdims=True)

    x = jnp.concatenate([l_refs[p][0, 0] for p in range(n_pages)], axis=0) * LOG2E
    xp = _split3(x)
    r_in = _dot3_right(xp, ls_ref[...])
    tot_b = _dot3_right(xp, ones_ref[...])
    r_page = _dot3_left(m3_ref[...], _split3(tot_b))
    rp = _split3(r_in + r_page)
    bias_past = _dot3_left(eh_ref[...], rp)

    s_blocks = []
    for p in range(n_pages):
        kp_t = k_refs[p][0, 0].astype(BF16)
        s_blocks.append(_dot(qbd, kp_t) + bias_past[p * rows:(p + 1) * rows] + cn_col)
    pad = jnp.zeros((PAGE - t_new, WIDTH_A), F32)
    kn = jnp.concatenate([kn_ref[...], pad], axis=0).astype(BF16)
    vn = jnp.concatenate([vn_ref[...], pad], axis=0).astype(BF16)
    cn_pad = jnp.concatenate([cn, jnp.zeros((PAGE - t_new, LANES), F32)], axis=0)
    cnp = _split3(cn_pad)
    ehh = ehh_ref[...]
    cn_keys = _dot_nt(ehh, cnp[0]) + _dot_nt(ehh, cnp[1]) + _dot_nt(ehh, cnp[2])
    s_new = _dot_nt(qbd, kn) + cn_col - cn_keys
    s_new = jnp.where((lane1 <= (row1 & (t_new - 1))) & (lane1 < t_new), s_new, NEG_BIG)
    s_blocks.append(s_new)

    m = s_blocks[0]
    for sb in s_blocks[1:]:
        m = jnp.maximum(m, sb)
    m = jnp.max(m, axis=1, keepdims=True)
    l_sum = jnp.zeros((rows, 1), F32)
    o = jnp.zeros((rows, WIDTH_A), F32)
    for p in range(n_pages + 1):
        pe = jnp.exp2(s_blocks[p] - m)
        l_sum = l_sum + jnp.sum(pe, axis=1, keepdims=True)
        if p < n_pages:
            o = o + _dot_nt(pe.astype(BF16), v_refs[p][0, 0].astype(BF16))
        else:
            o = o + _dot(pe.astype(BF16), vn)
    o = o / l_sum
    lane_h = lax.broadcasted_iota(jnp.int32, (t_new, WIDTH_A), 1) >> 6
    out = jnp.zeros((t_new, WIDTH_A), F32)
    for hd in range(N_HEADS):
        out = out + jnp.where(lane_h == hd, o[hd * t_new:(hd + 1) * t_new], 0.0)
    o_ref[...] = out


def _attn_sample(layer, page_table, q, kn, vn, lnp, ck, cv, clt, t_new):
    n_seq, n_pages = page_table.shape
    rows = N_HEADS * t_new
    pr = n_pages * N_HEADS
    key = np.arange(PAGE)
    ls = (key[:, None] > key[None, :]).astype(np.float32)
    ones = np.ones((PAGE, LANES), np.float32)
    ph = np.arange(pr)
    m3 = ((ph[:, None] % N_HEADS == ph[None, :] % N_HEADS)
          & (ph[None, :] // N_HEADS > ph[:, None] // N_HEADS)).astype(np.float32)
    r_all = np.arange(n_pages * rows)
    eh = ((r_all[:, None] // rows == ph[None, :] // N_HEADS)
          & ((r_all[:, None] % rows) // t_new == ph[None, :] % N_HEADS)).astype(np.float32)
    ehh = ((np.arange(rows)[:, None] // t_new) == np.arange(LANES)[None, :]).astype(np.float32)
    consts = tuple(jnp.asarray(a, BF16) for a in (ls, ones, m3, eh, ehh))

    tok = lambda width: pl.BlockSpec((t_new, width), lambda b, pt: (b, 0))
    def page_specs(block):
        return [pl.BlockSpec(block, functools.partial(lambda b, pt, p: (layer, pt[b, p], 0, 0), p=p))
                for p in range(n_pages)]
    grid_spec = pltpu.PrefetchScalarGridSpec(
        num_scalar_prefetch=1,
        grid=(n_seq,),
        in_specs=[tok(WIDTH_A), tok(WIDTH_A), tok(WIDTH_A), tok(LANES)]
        + [pl.BlockSpec(c.shape, lambda b, pt: (0, 0)) for c in consts]
        + page_specs((1, 1, WIDTH_A, PAGE)) + page_specs((1, 1, WIDTH_A, PAGE))
        + page_specs((1, 1, N_HEADS, PAGE)),
        out_specs=tok(WIDTH_A),
    )
    return pl.pallas_call(
        functools.partial(_attn_sample_kernel, n_pages, t_new),
        out_shape=jax.ShapeDtypeStruct((n_seq * t_new, WIDTH_A), F32),
        grid_spec=grid_spec,
        compiler_params=pltpu.CompilerParams(
            dimension_semantics=("arbitrary",), vmem_limit_bytes=VMEM_LIMIT),
        name="attn_sample",
    )(page_table, q, kn, vn, lnp, *consts, *([ck] * n_pages), *([cv] * n_pages), *([clt] * n_pages))


def _merge_kernel(transposed, a_ref, ga_ref, bm_ref, sa_ref, sb_ref, x_ref, wpa_ref, wpb_ref, wo_ref, y_ref):
    tm = x_ref.shape[-2]
    attn = a_ref[0].T if transposed else a_ref[...]
    ga = ga_ref[...].reshape(tm, WIDTH_A)
    a = (attn * ga.astype(F32)).astype(BF16)
    pa = _dot(a, wpa_ref[...])
    pb = _dot(bm_ref[...].reshape(tm, WIDTH_B), wpb_ref[...])
    m = sa_ref[...].reshape(tm, D_MODEL).astype(F32) * pa + sb_ref[...].reshape(tm, D_MODEL).astype(F32) * pb
    y = x_ref[...].reshape(tm, D_MODEL) + _dot(m.astype(BF16), wo_ref[...])
    y_ref[...] = y.reshape(y_ref.shape)


def _merge_prompt(attn_t, ga, bm, sa, sb, x, wpa, wpb, wo, tm):
    b, s, _ = x.shape
    row3 = lambda width: pl.BlockSpec((1, tm, width), lambda bi, i: (bi, i, 0))
    return pl.pallas_call(
        functools.partial(_merge_kernel, True),
        out_shape=jax.ShapeDtypeStruct((b, s, D_MODEL), F32),
        grid=(b, s // tm),
        in_specs=[pl.BlockSpec((1, WIDTH_A, tm), lambda bi, i: (bi, 0, i)),
                  row3(WIDTH_A), row3(WIDTH_B), row3(D_MODEL), row3(D_MODEL), row3(D_MODEL),
                  _const_spec(wpa.shape), _const_spec(wpb.shape), _const_spec(wo.shape)],
        out_specs=row3(D_MODEL),
        compiler_params=pltpu.CompilerParams(
            dimension_semantics=("arbitrary", "arbitrary"), vmem_limit_bytes=VMEM_LIMIT),
        name="merge_prompt",
    )(attn_t, ga, bm, sa, sb, x, wpa, wpb, wo)


def _merge_sample(attn, ga, bm, sa, sb, x, wpa, wpb, wo, tm):
    n = x.shape[0]
    row2 = lambda width: pl.BlockSpec((tm, width), lambda i: (i, 0))
    return pl.pallas_call(
        functools.partial(_merge_kernel, False),
        out_shape=jax.ShapeDtypeStruct((n, D_MODEL), F32),
        grid=(n // tm,),
        in_specs=[row2(WIDTH_A), row2(WIDTH_A), row2(WIDTH_B), row2(D_MODEL), row2(D_MODEL), row2(D_MODEL),
                  _const_spec(wpa.shape), _const_spec(wpb.shape), _const_spec(wo.shape)],
        out_specs=row2(D_MODEL),
        compiler_params=pltpu.CompilerParams(
            dimension_semantics=("arbitrary",), vmem_limit_bytes=VMEM_LIMIT),
        name="merge_sample",
    )(attn, ga, bm, sa, sb, x, wpa, wpb, wo)


TM_PROMPT = 256
HEADS_PER_STEP = 8
TM_MERGE = 512
TM_SAMPLE = 256


def kernel(x_prompt, x_sample, cache_k, cache_v, cache_logf, page_table, norm_g, w_in, b_forget, q_norm_g,
           k_norm_g, v_norm_g, w_s, b_s, w_proj_a, w_proj_b, w_out):
    depth = w_in.shape[0]
    bsz, seq, _ = x_prompt.shape
    n_seq, t_new, _ = x_sample.shape
    n_pool = cache_k.shape[1]

    f_lo = 4 * WIDTH_A
    w_perm = jnp.concatenate(
        [w_in[:, :, :f_lo], w_in[:, :, f_lo + N_HEADS:], w_in[:, :, f_lo:f_lo + N_HEADS],
         jnp.zeros((depth, D_MODEL, C_END - C_F - N_HEADS), w_in.dtype)], axis=-1).astype(BF16)
    bf_pad = jnp.pad(b_forget, ((0, 0), (0, LANES - N_HEADS)))[:, None, :]
    qg = jnp.tile(q_norm_g, (1, N_HEADS))[:, None, :]
    kg = jnp.tile(k_norm_g, (1, N_HEADS))[:, None, :]
    vg = v_norm_g[:, None, :]
    ng = norm_g[:, None, :]
    wcat_p = w_s.transpose(0, 2, 1, 3).reshape(depth, CHUNK, N_GROUPS * CHUNK).astype(BF16)
    w_small = jnp.tile(w_s[:, :, :t_new, :t_new], (1, 1, CHUNK // t_new, CHUNK // t_new))
    wcat_s = w_small.transpose(0, 2, 1, 3).reshape(depth, CHUNK, N_GROUPS * CHUNK).astype(BF16)
    sbias_p = jnp.repeat(b_s.transpose(0, 2, 1), GROUP_DIM, axis=-1)
    sbias_s = jnp.repeat(jnp.tile(b_s[:, :, :t_new], (1, 1, CHUNK // t_new)).transpose(0, 2, 1),
                         GROUP_DIM, axis=-1)
    wpa = w_proj_a.astype(BF16)
    wpb = w_proj_b.astype(BF16)
    wo = w_out.astype(BF16)
    ck = cache_k.transpose(0, 1, 3, 4, 2).reshape(depth, n_pool, WIDTH_A, PAGE)
    cv = cache_v.transpose(0, 1, 3, 4, 2).reshape(depth, n_pool, WIDTH_A, PAGE)
    clt = cache_logf.transpose(0, 1, 3, 2)

    xp = x_prompt
    xs = x_sample.reshape(n_seq * t_new, D_MODEL)
    outs = [[] for _ in range(4)]
    stacked = None
    for l in range(depth):
        common = (ng[l], w_perm[l], bf_pad[l], qg[l], kg[l], vg[l])
        qt, ka, vt, *stacked, ga, bm, sa, sb = _in_proj_prompt(
            xp, common + (wcat_p[l], sbias_p[l]), TM_PROMPT, l, depth, stacked)
        attn_t = _attn_prompt(qt, ka, vt, HEADS_PER_STEP)
        xp = _merge_prompt(attn_t, ga, bm, sa, sb, xp, wpa[l], wpb[l], wo[l], TM_MERGE)

        q_s, k_s, v_s, lf_s, lfp_s, vbn_s, ga, bm, sa, sb = _in_proj_sample(
            xs, common + (wcat_s[l], sbias_s[l]), TM_SAMPLE, t_new)
        attn_s = _attn_sample(l, page_table, q_s, k_s, v_s, lfp_s, ck, cv, clt, t_new)
        xs = _merge_sample(attn_s, ga, bm, sa, sb, xs, wpa[l], wpb[l], wo[l], TM_SAMPLE)
        outs[0].append(k_s.reshape(n_seq, t_new, N_HEADS, HEAD_DIM))
        outs[1].append(v_s.reshape(n_seq, t_new, N_HEADS, HEAD_DIM))
        outs[2].append(lf_s.reshape(n_seq, t_new, N_HEADS))
        outs[3].append(vbn_s.reshape(n_seq, t_new, WIDTH_B))
    kt_all, vt_all, lft_all = stacked
    unstack = lambda a: a.reshape(depth, bsz, N_HEADS, HEAD_DIM, seq).transpose(0, 1, 4, 2, 3)
    return ((xp, xs.reshape(n_seq, t_new, D_MODEL), unstack(kt_all), unstack(vt_all),
             lft_all.transpose(0, 1, 3, 2)) + tuple(jnp.stack(o) for o in outs))
```

```python
import functools
import math

import numpy as np
import jax
import jax.numpy as jnp
from jax import lax
from jax.experimental import pallas as pl
from jax.experimental.pallas import tpu as pltpu

D_MODEL = 1024
N_HEADS = 8
HEAD_DIM = 64
HEAD_BITS = 3
HEAD_DIM_BITS = 6
WIDTH_A = N_HEADS * HEAD_DIM
N_GROUPS = 8
GROUP_DIM = 64
GROUP_DIM_BITS = 6
WIDTH_B = N_GROUPS * GROUP_DIM
CHUNK = 128
PAGE = 128
EPS = 1e-6
LOG2E = math.log2(math.e)
NEG_BIG = -1e30

LANES = 128
VT_ROWS = 80
VMEM_LIMIT = 56 * 1024 * 1024

C_Q, C_K, C_V, C_GA, C_U, C_VB, C_GB, C_SA, C_SB, C_F, C_END = (
    0, 512, 1024, 1536, 2048, 2560, 3072, 3584, 4608, 5632, 5760)

F32 = jnp.float32
BF16 = jnp.bfloat16


def _split3(x):
    hi = x.astype(BF16)
    r1 = x - hi.astype(F32)
    mid = r1.astype(BF16)
    lo = (r1 - mid.astype(F32)).astype(BF16)
    return hi, mid, lo


def _dot(a, b):
    return jnp.dot(a, b, preferred_element_type=F32)


def _dot_nt(a, b):
    return lax.dot_general(a, b, (((1,), (1,)), ((), ())), preferred_element_type=F32)


def _dot3_right(parts, w):
    return _dot(parts[0], w) + _dot(parts[1], w) + _dot(parts[2], w)


def _dot3_left(w, parts):
    return _dot(w, parts[0]) + _dot(w, parts[1]) + _dot(w, parts[2])


def _sigmoid(x):
    return 1.0 / (1.0 + jnp.exp(-x))


def _gelu_tanh(x):
    c = math.sqrt(2.0 / math.pi)
    return 0.5 * x * (1.0 + jnp.tanh(c * (x + 0.044715 * (x * x * x))))


def _log_sigmoid(x):
    return jnp.minimum(x, 0.0) - jnp.log1p(jnp.exp(-jnp.abs(x)))


def _in_proj_common(x, g_ref, w_ref, bf_ref, qg_ref, kg_ref, vg_ref, bd_ref, wcat_ref, sbias_ref,
                    seq_local):
    tm = x.shape[0]
    ms = jnp.mean(x * x, axis=-1, keepdims=True)
    h = ((x * lax.rsqrt(ms + EPS)) * g_ref[...]).astype(BF16)

    def seg(lo, hi):
        w_head, w_tail, w_forget = w_ref
        if hi <= C_U:
            return _dot(h, w_head[:, lo:hi])
        if lo >= C_F:
            return _dot(h, w_forget[...])
        return _dot(h, w_tail[:, lo - C_U:hi - C_U])

    bd = bd_ref[...]

    def head_norm(z, gain):
        msz = _dot((z * z).astype(BF16), bd) * (1.0 / HEAD_DIM)
        return z * lax.rsqrt(msz + EPS) * gain

    qn = head_norm(seg(C_Q, C_K), qg_ref[...])
    kn = head_norm(seg(C_K, C_V), kg_ref[...])
    v = seg(C_V, C_GA)
    zga = seg(C_GA, C_U)
    ga = zga * _sigmoid(zga)

    lane = lax.broadcasted_iota(jnp.int32, (tm, LANES), 1)
    logf = jnp.where(lane < N_HEADS, _log_sigmoid(seg(C_F, C_END) + bf_ref[...]), 0.0)

    u = _gelu_tanh(seg(C_U, C_VB))
    vbg = _gelu_tanh(seg(C_VB, C_GB))
    msv = jnp.mean(vbg * vbg, axis=-1, keepdims=True)
    vbn = vbg * lax.rsqrt(msv + EPS) * vg_ref[...]
    zgb = seg(C_GB, C_SA)
    ugb = u * (zgb * _sigmoid(zgb))

    t_idx = lax.broadcasted_iota(jnp.int32, (CHUNK, N_GROUPS * CHUNK), 0)
    s_idx = lax.broadcasted_iota(jnp.int32, (CHUNK, N_GROUPS * CHUNK), 1) & (CHUNK - 1)
    keep = s_idx <= t_idx
    if seq_local < CHUNK:
        shift = int(math.log2(seq_local))
        keep = keep & ((s_idx >> shift) == (t_idx >> shift))
    wm = jnp.where(keep, wcat_ref[...], jnp.zeros((), BF16))
    gq = N_GROUPS // 2
    glane = lax.broadcasted_iota(jnp.int32, (CHUNK, gq * GROUP_DIM), 1) >> GROUP_DIM_BITS
    vbn_b = vbn.astype(BF16)
    s_chunks = []
    for c in range(tm // CHUNK):
        halves = []
        for hf in range(2):
            vc = vbn_b[c * CHUNK:(c + 1) * CHUNK, hf * gq * GROUP_DIM:(hf + 1) * gq * GROUP_DIM]
            rhs = jnp.concatenate(
                [jnp.where(glane == g, vc, jnp.zeros((), BF16)) for g in range(gq)], axis=0)
            halves.append(_dot(wm[:, hf * gq * CHUNK:(hf + 1) * gq * CHUNK], rhs))
        s_chunks.append(jnp.concatenate(halves, axis=1) + sbias_ref[...])
    s_sgu = jnp.concatenate(s_chunks, axis=0)
    bmix = ugb * s_sgu

    sa = _sigmoid(seg(C_SA, C_SB))
    sb = _sigmoid(seg(C_SB, C_F))
    return qn, kn, v, logf, vbn, ga, bmix, sa, sb


def _in_proj_prompt_kernel(n_alias, *refs):
    (x_ref, g_ref, w_ref, bf_ref, qg_ref, kg_ref, vg_ref, bd_ref, wcat_ref, sbias_ref, tri_ref, pk_ref,
     pq_ref, qt_ref, ka_ref, vt_ref, ko_ref, vo_ref, lf_ref, ga_ref, bm_ref, sa_ref, sb_ref,
     carry_ref) = refs[n_alias:]
    sr = vt_ref.shape[-1]

    @pl.when(pl.program_id(1) == 0)
    def _():
        carry_ref[...] = jnp.zeros_like(carry_ref)

    carry = carry_ref[0:1, :]
    lane = lax.broadcasted_iota(jnp.int32, (sr, LANES), 1)
    ones_blk = jnp.where(lax.broadcasted_iota(jnp.int32, (VT_ROWS - HEAD_DIM, sr), 0) == 0, 1.0, 0.0)
    for sub in range(x_ref.shape[-2] // sr):
        rows = slice(sub * sr, (sub + 1) * sr)
        qn, kn, v, logf, _, ga, bmix, sa, sb = _in_proj_common(
            x_ref[0, rows], g_ref, w_ref, bf_ref, qg_ref, kg_ref, vg_ref, bd_ref, wcat_ref, sbias_ref, CHUNK)

        lf_ref[0, 0, :, rows] = logf.T[0:N_HEADS]
        ga_ref[0, rows] = ga.astype(BF16)
        bm_ref[0, rows] = bmix.astype(BF16)
        sa_ref[0, rows] = sa.astype(BF16)
        sb_ref[0, rows] = sb.astype(BF16)

        c = _dot3_left(tri_ref[...], _split3(logf)) + carry
        carry = c[sr - 1:sr, :]
        c2 = c * LOG2E

        hi, mid, lo = _split3(c2)
        cp = (hi.astype(F32) + pltpu.roll(mid.astype(F32), 8, axis=1)
              + pltpu.roll(lo.astype(F32), 16, axis=1) + jnp.where(lane == 24, 1.0, 0.0))
        extra_k = _dot(cp.astype(BF16), pk_ref[...])

        c2t = c2.T
        hit, midt, lot = (part.astype(F32) for part in _split3(c2t[0:8]))
        row = lax.broadcasted_iota(jnp.int32, (8, sr), 0)
        cpt = jnp.concatenate(
            [hit, midt, lot, jnp.where(row == 0, 1.0, 0.0), jnp.zeros((LANES - 32, sr), F32)],
            axis=0).astype(BF16)
        extra_q = _dot(pq_ref[...], cpt)

        q2 = qn * (HEAD_DIM ** -0.5 * LOG2E)
        for j in range(N_HEADS // 2):
            qs_t = q2[:, j * LANES:(j + 1) * LANES].T
            vs_t = v[:, j * LANES:(j + 1) * LANES].T
            ks = kn[:, j * LANES:(j + 1) * LANES]
            ks_sw = pltpu.roll(ks, HEAD_DIM, axis=1)
            ko_ref[0, 0, j * LANES:(j + 1) * LANES, rows] = ks.T
            vo_ref[0, 0, j * LANES:(j + 1) * LANES, rows] = vs_t
            for hh in range(2):
                hd = 2 * j + hh
                qt_ref[0, hd, :, rows] = jnp.concatenate(
                    [qs_t[hh * HEAD_DIM:(hh + 1) * HEAD_DIM], extra_q[hd * HEAD_DIM:(hd + 1) * HEAD_DIM]],
                    axis=0).astype(BF16)
                vt_ref[0, hd, sub] = jnp.concatenate(
                    [vs_t[hh * HEAD_DIM:(hh + 1) * HEAD_DIM], ones_blk], axis=0).astype(BF16)
                ka_ref[0, hd, rows] = jnp.where(lane < HEAD_DIM, ks if hh == 0 else ks_sw,
                                                extra_k).astype(BF16)
    carry_ref[0:1, :] = carry


def _in_proj_sample_kernel(x_ref, g_ref, w_ref, bf_ref, qg_ref, kg_ref, vg_ref, bd_ref, wcat_ref,
                           sbias_ref,
                           q_ref, ko_ref, vo_ref, lf_ref, lfp_ref, vbn_ref, ga_ref, bm_ref, sa_ref, sb_ref,
                           *, seq_local):
    qn, kn, v, logf, vbn, ga, bmix, sa, sb = _in_proj_common(
        x_ref[...], g_ref, w_ref, bf_ref, qg_ref, kg_ref, vg_ref, bd_ref, wcat_ref, sbias_ref, seq_local)
    q_ref[...] = qn * (HEAD_DIM ** -0.5 * LOG2E)
    ko_ref[...] = kn
    vo_ref[...] = v
    lf_ref[...] = logf[:, :N_HEADS]
    lfp_ref[...] = logf
    vbn_ref[...] = vbn
    ga_ref[...] = ga.astype(BF16)
    bm_ref[...] = bmix.astype(BF16)
    sa_ref[...] = sa.astype(BF16)
    sb_ref[...] = sb.astype(BF16)


def _const_spec(arr):
    if isinstance(arr, tuple):
        return tuple(_const_spec(a) for a in arr)
    return pl.BlockSpec(arr.shape, lambda *_: (0,) * arr.ndim, pipeline_mode=pl.Buffered(1))


def _in_proj_consts(tm):
    bd = np.kron(np.eye(N_HEADS, dtype=np.float32), np.ones((HEAD_DIM, HEAD_DIM), np.float32))
    tri = np.tril(np.ones((tm, tm), np.float32))
    pk = np.zeros((LANES, LANES), np.float32)
    pq = np.zeros((N_HEADS * HEAD_DIM, LANES), np.float32)
    for hd in range(N_HEADS):
        for part in range(3):
            pk[8 * part + hd, HEAD_DIM + 3 * hd + part] = -1.0
            pk[24, HEAD_DIM + 3 * N_HEADS + part] = 1.0
            pq[hd * HEAD_DIM + 3 * hd + part, 24] = 1.0
            pq[hd * HEAD_DIM + 3 * N_HEADS + part, 8 * part + hd] = 1.0
    return (jnp.asarray(bd, BF16), jnp.asarray(tri, BF16), jnp.asarray(pk, BF16), jnp.asarray(pq, BF16))


def _in_proj_prompt(x, params, tm, slab, layer, depth, stacked):
    b, s, _ = x.shape
    nt = s // tm
    bd, tri, pk, pq = _in_proj_consts(slab)
    g, w, bf, qg, kg, vg, wcat, sbias = params
    consts = (g, w, bf, qg, kg, vg, bd, wcat, sbias, tri, pk, pq)
    stacked = () if stacked is None else tuple(stacked)
    row3 = lambda width: pl.BlockSpec((1, tm, width), lambda bi, i: (bi, i, 0))
    col4 = lambda rows: pl.BlockSpec((1, 1, rows, tm), lambda bi, i: (layer, bi, 0, i))
    out_shape = (
        jax.ShapeDtypeStruct((b, N_HEADS, LANES, s), BF16),
        jax.ShapeDtypeStruct((b, N_HEADS, s, LANES), BF16),
        jax.ShapeDtypeStruct((b, N_HEADS, s // slab, VT_ROWS, slab), BF16),
        jax.ShapeDtypeStruct((depth, b, WIDTH_A, s), F32),
        jax.ShapeDtypeStruct((depth, b, WIDTH_A, s), F32),
        jax.ShapeDtypeStruct((depth, b, N_HEADS, s), F32),
        jax.ShapeDtypeStruct((b, s, WIDTH_A), BF16),
        jax.ShapeDtypeStruct((b, s, WIDTH_B), BF16),
        jax.ShapeDtypeStruct((b, s, D_MODEL), BF16),
        jax.ShapeDtypeStruct((b, s, D_MODEL), BF16),
    )
    out_specs = (
        pl.BlockSpec((1, N_HEADS, LANES, tm), lambda bi, i: (bi, 0, 0, i)),
        pl.BlockSpec((1, N_HEADS, tm, LANES), lambda bi, i: (bi, 0, i, 0)),
        pl.BlockSpec((1, N_HEADS, tm // slab, VT_ROWS, slab), lambda bi, i: (bi, 0, i, 0, 0)),
        col4(WIDTH_A), col4(WIDTH_A), col4(N_HEADS), row3(WIDTH_A), row3(WIDTH_B),
        row3(D_MODEL), row3(D_MODEL),
    )
    return pl.pallas_call(
        functools.partial(_in_proj_prompt_kernel, len(stacked)),
        out_shape=out_shape,
        grid=(b, nt),
        in_specs=[pl.BlockSpec(memory_space=pl.ANY)] * len(stacked) + [row3(D_MODEL)]
        + [_const_spec(c) for c in consts],
        out_specs=out_specs,
        scratch_shapes=[pltpu.VMEM((8, LANES), F32)],
        input_output_aliases={n: 3 + n for n in range(len(stacked))},
        compiler_params=pltpu.CompilerParams(
            dimension_semantics=("arbitrary", "arbitrary"), vmem_limit_bytes=VMEM_LIMIT),
        name="in_proj_prompt",
    )(*stacked, x, *consts)


def _in_proj_sample(x, params, tm, seq_local):
    n = x.shape[0]
    bd = _in_proj_consts(tm)[0]
    g, w, bf, qg, kg, vg, wcat, sbias = params
    consts = (g, w, bf, qg, kg, vg, bd, wcat, sbias)
    row2 = lambda width: pl.BlockSpec((tm, width), lambda i: (i, 0))
    out_shape = (
        jax.ShapeDtypeStruct((n, WIDTH_A), F32),
        jax.ShapeDtypeStruct((n, WIDTH_A), F32),
        jax.ShapeDtypeStruct((n, WIDTH_A), F32),
        jax.ShapeDtypeStruct((n, N_HEADS), F32),
        jax.ShapeDtypeStruct((n, LANES), F32),
        jax.ShapeDtypeStruct((n, WIDTH_B), F32),
        jax.ShapeDtypeStruct((n, WIDTH_A), BF16),
        jax.ShapeDtypeStruct((n, WIDTH_B), BF16),
        jax.ShapeDtypeStruct((n, D_MODEL), BF16),
        jax.ShapeDtypeStruct((n, D_MODEL), BF16),
    )
    out_specs = (row2(WIDTH_A), row2(WIDTH_A), row2(WIDTH_A), row2(N_HEADS), row2(LANES), row2(WIDTH_B),
                 row2(WIDTH_A), row2(WIDTH_B), row2(D_MODEL), row2(D_MODEL))
    return pl.pallas_call(
        functools.partial(_in_proj_sample_kernel, seq_local=seq_local),
        out_shape=out_shape,
        grid=(n // tm,),
        in_specs=[row2(D_MODEL)] + [_const_spec(c) for c in consts],
        out_specs=out_specs,
        compiler_params=pltpu.CompilerParams(
            dimension_semantics=("arbitrary",), vmem_limit_bytes=VMEM_LIMIT),
        name="in_proj_sample",
    )(x, *consts)


def _attn_prompt_kernel(qt_ref, ka_ref, vt_ref, o_ref, s_ref, m_ref, al_ref, acc_ref):
    g = qt_ref.shape[1]
    tq = qt_ref.shape[-1]
    t = vt_ref.shape[-1]
    assert tq == 2 * t
    i = pl.program_id(2)
    key = lax.broadcasted_iota(jnp.int32, (t, tq), 0)
    qry = lax.broadcasted_iota(jnp.int32, (t, tq), 1)

    def scores(hd, blk, slot, mask, m_old):
        k = ka_ref[0, hd, pl.ds(pl.multiple_of(blk * t, t), t), :]
        s = _dot(k, qt_ref[0, hd])
        if mask is not None:
            s = jnp.where(mask, s, NEG_BIG)
        s_ref[slot, hd] = s
        m_new = jnp.maximum(m_old, jnp.max(s, axis=0, keepdims=True))
        m_ref[slot, hd, 0:1, :] = m_new
        al_ref[slot, hd, 0:1, :] = jnp.exp2(m_old - m_new)

    def iteration(j, cur, next_mask, has_next):
        for hd in range(g):
            p = jnp.exp2(s_ref[cur, hd] - m_ref[cur, hd, 0:1, :]).astype(BF16)
            if has_next:
                scores(hd, j + 1, 1 - cur, next_mask, m_ref[cur, hd, 0:1, :])
            acc_ref[hd] = al_ref[cur, hd, 0:1, :] * acc_ref[hd] + _dot(vt_ref[0, hd, j], p)

    acc_ref[...] = jnp.zeros_like(acc_ref)
    diag_lo = key <= qry
    diag_hi = key + t <= qry
    for hd in range(g):
        scores(hd, 0, 0, key <= qry + jnp.where(i > 0, tq, 0), jnp.full((1, tq), NEG_BIG, F32))

    def pair(jj, carry):
        iteration(2 * jj, 0, None, True)
        iteration(2 * jj + 1, 1, None, True)
        return carry

    lax.fori_loop(0, i - 1, pair, 0)

    @pl.when(i >= 1)
    def _():
        iteration(2 * i - 2, 0, None, True)
        iteration(2 * i - 1, 1, diag_lo, True)

    last = 2 * i + 1
    iteration(last - 1, 0, diag_hi, True)
    iteration(last, 1, None, False)
    for hd in range(g):
        acc = acc_ref[hd]
        o_ref[0, hd * HEAD_DIM:(hd + 1) * HEAD_DIM, :] = acc[0:HEAD_DIM] / acc[HEAD_DIM:HEAD_DIM + 1]


def _attn_prompt(qt, ka, vt, g):
    b, h, _, s = qt.shape
    t = vt.shape[-1]
    tq = 2 * t
    return pl.pallas_call(
        _attn_prompt_kernel,
        out_shape=jax.ShapeDtypeStruct((b, WIDTH_A, s), F32),
        grid=(b, h // g, s // tq),
        in_specs=[
            pl.BlockSpec((1, g, LANES, tq), lambda bi, hi, i: (bi, hi, 0, i)),
            pl.BlockSpec((1, g, s, LANES), lambda bi, hi, i: (bi, hi, 0, 0),
                         pipeline_mode=pl.Buffered(1)),
            pl.BlockSpec((1, g, s // t, VT_ROWS, t), lambda bi, hi, i: (bi, hi, 0, 0, 0),
                         pipeline_mode=pl.Buffered(1)),
        ],
        out_specs=pl.BlockSpec((1, g * HEAD_DIM, tq), lambda bi, hi, i: (bi, hi, i)),
        scratch_shapes=[pltpu.VMEM((2, g, t, tq), F32),
                        pltpu.VMEM((2, g, 8, tq), F32), pltpu.VMEM((2, g, 8, tq), F32),
                        pltpu.VMEM((g, VT_ROWS, tq), F32)],
        compiler_params=pltpu.CompilerParams(
            dimension_semantics=("arbitrary", "arbitrary", "arbitrary"), vmem_limit_bytes=VMEM_LIMIT),
        name="attn_prompt",
    )(qt, ka, vt)


def _attn_sample_kernel(n_pages, t_new, pt_ref, q_ref, kn_ref, vn_ref, ln_ref, ls_ref, ones_ref, m3_ref,
                        *rest):
    del pt_ref
    k_refs = rest[:n_pages]
    v_refs = rest[n_pages:2 * n_pages]
    l_refs = rest[2 * n_pages:3 * n_pages]
    o_ref = rest[3 * n_pages]
    rows = t_new * N_HEADS

    def per_token(x):
        return jnp.concatenate(
            [jnp.broadcast_to(x[tk:tk + 1], (N_HEADS, x.shape[1])) for tk in range(t_new)], axis=0)

    row = lax.broadcasted_iota(jnp.int32, (rows, WIDTH_A), 0)
    lane = lax.broadcasted_iota(jnp.int32, (rows, WIDTH_A), 1)
    own_head = (row & (N_HEADS - 1)) == (lane >> HEAD_DIM_BITS)
    qbd = jnp.where(own_head, per_token(q_ref[...]), 0.0).astype(BF16)

    cn = ln_ref[...] * LOG2E
    trow = lax.broadcasted_iota(jnp.int32, (t_new, LANES), 0)
    sh = 1
    while sh < t_new:
        cn = cn + jnp.where(trow >= sh, pltpu.roll(cn, sh, axis=0), 0.0)
        sh *= 2
    row1 = lax.broadcasted_iota(jnp.int32, (rows, LANES), 0)
    lane1 = lax.broadcasted_iota(jnp.int32, (rows, LANES), 1)
    head_lane = (row1 & (N_HEADS - 1)) == lane1
    cn_col = jnp.sum(jnp.where(head_lane, per_token(cn), 0.0), axis=1, keepdims=True)

    x = jnp.concatenate([l_refs[p][0, 0] for p in range(n_pages)], axis=0) * LOG2E
    xp = _split3(x)
    r_in = _dot3_right(xp, ls_ref[...])
    tot_b = _dot3_right(xp, ones_ref[...])
    r_page = _dot3_left(m3_ref[...], _split3(tot_b))
    r = r_in + r_page

    s_blocks = []
    for p in range(n_pages):
        kp_t = k_refs[p][0, 0].astype(BF16)
        r_p = r[p * N_HEADS:(p + 1) * N_HEADS]
        s_blocks.append(_dot(qbd, kp_t) + jnp.concatenate([r_p] * t_new, axis=0) + cn_col)
    pad = jnp.zeros((PAGE - t_new, WIDTH_A), F32)
    kn = jnp.concatenate([kn_ref[...], pad], axis=0).astype(BF16)
    vn = jnp.concatenate([vn_ref[...], pad], axis=0).astype(BF16)
    cn_pad = jnp.concatenate([cn, jnp.zeros((PAGE - t_new, LANES), F32)], axis=0)
    sel = jnp.where(head_lane, 1.0, 0.0).astype(BF16)
    cn_keys = sum(_dot_nt(sel, part) for part in _split3(cn_pad))
    s_new = _dot_nt(qbd, kn) + cn_col - cn_keys
    s_new = jnp.where((lane1 <= (row1 >> HEAD_BITS)) & (lane1 < t_new), s_new, NEG_BIG)
    s_blocks.append(s_new)

    m = s_blocks[0]
    for sb in s_blocks[1:]:
        m = jnp.maximum(m, sb)
    m = jnp.max(m, axis=1, keepdims=True)
    l_sum = jnp.zeros((rows, 1), F32)
    o = jnp.zeros((rows, WIDTH_A), F32)
    for p in range(n_pages + 1):
        pe = jnp.exp2(s_blocks[p] - m)
        l_sum = l_sum + jnp.sum(pe, axis=1, keepdims=True)
        if p < n_pages:
            o = o + _dot_nt(pe.astype(BF16), v_refs[p][0, 0].astype(BF16))
        else:
            o = o + _dot(pe.astype(BF16), vn)
    o = jnp.where(own_head, o / l_sum, 0.0)
    o_ref[...] = jnp.concatenate(
        [jnp.sum(o[tk * N_HEADS:(tk + 1) * N_HEADS], axis=0, keepdims=True) for tk in range(t_new)], axis=0)


def _attn_sample(layer, page_table, q, kn, vn, lnp, ck, cv, clt, t_new):
    n_seq, n_pages = page_table.shape
    rows = N_HEADS * t_new
    pr = n_pages * N_HEADS
    key = np.arange(PAGE)
    ls = (key[:, None] > key[None, :]).astype(np.float32)
    ones = np.ones((PAGE, LANES), np.float32)
    ph = np.arange(pr)
    m3 = ((ph[:, None] % N_HEADS == ph[None, :] % N_HEADS)
          & (ph[None, :] // N_HEADS > ph[:, None] // N_HEADS)).astype(np.float32)
    consts = tuple(jnp.asarray(a, BF16) for a in (ls, ones, m3))

    tok = lambda width: pl.BlockSpec((t_new, width), lambda b, pt: (b, 0))
    def page_specs(block):
        return [pl.BlockSpec(block, functools.partial(lambda b, pt, p: (layer, pt[b, p], 0, 0), p=p))
                for p in range(n_pages)]
    grid_spec = pltpu.PrefetchScalarGridSpec(
        num_scalar_prefetch=1,
        grid=(n_seq,),
        in_specs=[tok(WIDTH_A), tok(WIDTH_A), tok(WIDTH_A), tok(LANES)]
        + [pl.BlockSpec(c.shape, lambda b, pt: (0, 0)) for c in consts]
        + page_specs((1, 1, WIDTH_A, PAGE)) + page_specs((1, 1, WIDTH_A, PAGE))
        + page_specs((1, 1, N_HEADS, PAGE)),
        out_specs=tok(WIDTH_A),
    )
    return pl.pallas_call(
        functools.partial(_attn_sample_kernel, n_pages, t_new),
        out_shape=jax.ShapeDtypeStruct((n_seq * t_new, WIDTH_A), F32),
        grid_spec=grid_spec,
        compiler_params=pltpu.CompilerParams(
            dimension_semantics=("arbitrary",), vmem_limit_bytes=VMEM_LIMIT),
        name="attn_sample",
    )(page_table, q, kn, vn, lnp, *consts, *([ck] * n_pages), *([cv] * n_pages), *([clt] * n_pages))


def _merge_kernel(transposed, a_ref, ga_ref, bm_ref, sa_ref, sb_ref, x_ref, wpa_ref, wpb_ref, wo_ref, y_ref):
    tm = x_ref.shape[-2]
    attn = a_ref[0].T if transposed else a_ref[...]
    ga = ga_ref[...].reshape(tm, WIDTH_A)
    a = (attn * ga.astype(F32)).astype(BF16)
    pa = _dot(a, wpa_ref[...])
    pb = _dot(bm_ref[...].reshape(tm, WIDTH_B), wpb_ref[...])
    m = sa_ref[...].reshape(tm, D_MODEL).astype(F32) * pa + sb_ref[...].reshape(tm, D_MODEL).astype(F32) * pb
    y = x_ref[...].reshape(tm, D_MODEL) + _dot(m.astype(BF16), wo_ref[...])
    y_ref[...] = y.reshape(y_ref.shape)


def _merge_prompt(attn_t, ga, bm, sa, sb, x, wpa, wpb, wo, tm):
    b, s, _ = x.shape
    row3 = lambda width: pl.BlockSpec((1, tm, width), lambda bi, i: (bi, i, 0))
    return pl.pallas_call(
        functools.partial(_merge_kernel, True),
        out_shape=jax.ShapeDtypeStruct((b, s, D_MODEL), F32),
        grid=(b, s // tm),
        in_specs=[pl.BlockSpec((1, WIDTH_A, tm), lambda bi, i: (bi, 0, i)),
                  row3(WIDTH_A), row3(WIDTH_B), row3(D_MODEL), row3(D_MODEL), row3(D_MODEL),
                  _const_spec(wpa), _const_spec(wpb), _const_spec(wo)],
        out_specs=row3(D_MODEL),
        compiler_params=pltpu.CompilerParams(
            dimension_semantics=("arbitrary", "arbitrary"), vmem_limit_bytes=VMEM_LIMIT),
        name="merge_prompt",
    )(attn_t, ga, bm, sa, sb, x, wpa, wpb, wo)


def _merge_sample(attn, ga, bm, sa, sb, x, wpa, wpb, wo, tm):
    n = x.shape[0]
    row2 = lambda width: pl.BlockSpec((tm, width), lambda i: (i, 0))
    return pl.pallas_call(
        functools.partial(_merge_kernel, False),
        out_shape=jax.ShapeDtypeStruct((n, D_MODEL), F32),
        grid=(n // tm,),
        in_specs=[row2(WIDTH_A), row2(WIDTH_A), row2(WIDTH_B), row2(D_MODEL), row2(D_MODEL), row2(D_MODEL),
                  _const_spec(wpa), _const_spec(wpb), _const_spec(wo)],
        out_specs=row2(D_MODEL),
        compiler_params=pltpu.CompilerParams(
            dimension_semantics=("arbitrary",), vmem_limit_bytes=VMEM_LIMIT),
        name="merge_sample",
    )(attn, ga, bm, sa, sb, x, wpa, wpb, wo)


KEY_BLOCK = 256
TM_PROMPT = 512
HEADS_PER_STEP = 8
TM_MERGE = 512
TM_SAMPLE = 256


def kernel(x_prompt, x_sample, cache_k, cache_v, cache_logf, page_table, norm_g, w_in, b_forget, q_norm_g,
           k_norm_g, v_norm_g, w_s, b_s, w_proj_a, w_proj_b, w_out):
    depth = w_in.shape[0]
    bsz, seq, _ = x_prompt.shape
    n_seq, t_new, _ = x_sample.shape
    n_pool = cache_k.shape[1]

    f_lo = 4 * WIDTH_A
    w_head = w_in[:, :, :f_lo].astype(BF16)
    w_tail = w_in[:, :, f_lo + N_HEADS:].astype(BF16)
    w_forget = jnp.pad(w_in[:, :, f_lo:f_lo + N_HEADS],
                       ((0, 0), (0, 0), (0, C_END - C_F - N_HEADS))).astype(BF16)
    bf_pad = jnp.pad(b_forget, ((0, 0), (0, LANES - N_HEADS)))[:, None, :]
    qg = jnp.tile(q_norm_g, (1, N_HEADS))[:, None, :]
    kg = jnp.tile(k_norm_g, (1, N_HEADS))[:, None, :]
    vg = v_norm_g[:, None, :]
    ng = norm_g[:, None, :]
    wcat_p = w_s.transpose(0, 2, 1, 3).reshape(depth, CHUNK, N_GROUPS * CHUNK).astype(BF16)
    w_small = jnp.tile(w_s[:, :, :t_new, :t_new], (1, 1, CHUNK // t_new, CHUNK // t_new))
    wcat_s = w_small.transpose(0, 2, 1, 3).reshape(depth, CHUNK, N_GROUPS * CHUNK).astype(BF16)
    sbias_p = jnp.repeat(b_s.transpose(0, 2, 1), GROUP_DIM, axis=-1)
    sbias_s = jnp.repeat(jnp.tile(b_s[:, :, :t_new], (1, 1, CHUNK // t_new)).transpose(0, 2, 1),
                         GROUP_DIM, axis=-1)
    wpa = w_proj_a.astype(BF16)
    wpb = w_proj_b.astype(BF16)
    wo = w_out.astype(BF16)
    ck = cache_k.transpose(0, 1, 3, 4, 2).reshape(depth, n_pool, WIDTH_A, PAGE)
    cv = cache_v.transpose(0, 1, 3, 4, 2).reshape(depth, n_pool, WIDTH_A, PAGE)
    clt = cache_logf.transpose(0, 1, 3, 2)

    xp = x_prompt
    xs = x_sample.reshape(n_seq * t_new, D_MODEL)
    outs = [[] for _ in range(4)]
    stacked = None
    for l in range(depth):
        common = (ng[l], (w_head[l], w_tail[l], w_forget[l]), bf_pad[l], qg[l], kg[l], vg[l])
        qt, ka, vt, *stacked, ga, bm, sa, sb = _in_proj_prompt(
            xp, common + (wcat_p[l], sbias_p[l]), TM_PROMPT, KEY_BLOCK, l, depth, stacked)
        attn_t = _attn_prompt(qt, ka, vt, HEADS_PER_STEP)
        xp = _merge_prompt(attn_t, ga, bm, sa, sb, xp, wpa[l], wpb[l], wo[l], TM_MERGE)

        q_s, k_s, v_s, lf_s, lfp_s, vbn_s, ga, bm, sa, sb = _in_proj_sample(
            xs, common + (wcat_s[l], sbias_s[l]), TM_SAMPLE, t_new)
        attn_s = _attn_sample(l, page_table, q_s, k_s, v_s, lfp_s, ck, cv, clt, t_new)
        xs = _merge_sample(attn_s, ga, bm, sa, sb, xs, wpa[l], wpb[l], wo[l], TM_SAMPLE)
        outs[0].append(k_s.reshape(n_seq, t_new, N_HEADS, HEAD_DIM))
        outs[1].append(v_s.reshape(n_seq, t_new, N_HEADS, HEAD_DIM))
        outs[2].append(lf_s.reshape(n_seq, t_new, N_HEADS))
        outs[3].append(vbn_s.reshape(n_seq, t_new, WIDTH_B))
    kt_all, vt_all, lft_all = stacked
    unstack = lambda a: a.reshape(depth, bsz, N_HEADS, HEAD_DIM, seq).transpose(0, 1, 4, 2, 3)
    return ((xp, xs.reshape(n_seq, t_new, D_MODEL), unstack(kt_all), unstack(vt_all),
             lft_all.transpose(0, 1, 3, 2)) + tuple(jnp.stack(o) for o in outs))
```

```python
import functools
import math

import numpy as np
import jax
import jax.numpy as jnp
from jax import lax
from jax.experimental import pallas as pl
from jax.experimental.pallas import tpu as pltpu

D_MODEL = 1024
N_HEADS = 8
HEAD_DIM = 64
HEAD_BITS = 3
HEAD_DIM_BITS = 6
WIDTH_A = N_HEADS * HEAD_DIM
N_GROUPS = 8
GROUP_DIM = 64
GROUP_DIM_BITS = 6
WIDTH_B = N_GROUPS * GROUP_DIM
CHUNK = 128
PAGE = 128
EPS = 1e-6
LOG2E = math.log2(math.e)
NEG_BIG = -1e30

LANES = 128
VT_ROWS = 80
VMEM_LIMIT = 56 * 1024 * 1024

C_Q, C_K, C_V, C_GA, C_U, C_VB, C_GB, C_SA, C_SB, C_F, C_END = (
    0, 512, 1024, 1536, 2048, 2560, 3072, 3584, 4608, 5632, 5760)

F32 = jnp.float32
BF16 = jnp.bfloat16


def _split3(x):
    hi = x.astype(BF16)
    r1 = x - hi.astype(F32)
    mid = r1.astype(BF16)
    lo = (r1 - mid.astype(F32)).astype(BF16)
    return hi, mid, lo


def _dot(a, b):
    return jnp.dot(a, b, preferred_element_type=F32)


def _dot_nt(a, b):
    return lax.dot_general(a, b, (((1,), (1,)), ((), ())), preferred_element_type=F32)


def _dot3_right(parts, w):
    return _dot(parts[0], w) + _dot(parts[1], w) + _dot(parts[2], w)


def _dot3_left(w, parts):
    return _dot(w, parts[0]) + _dot(w, parts[1]) + _dot(w, parts[2])


def _sigmoid(x):
    return 1.0 / (1.0 + jnp.exp(-x))


def _gelu_tanh(x):
    c = math.sqrt(2.0 / math.pi)
    return 0.5 * x * (1.0 + jnp.tanh(c * (x + 0.044715 * (x * x * x))))


def _log_sigmoid(x):
    return jnp.minimum(x, 0.0) - jnp.log1p(jnp.exp(-jnp.abs(x)))


def _in_proj_common(x, g_ref, w_ref, bf_ref, qg_ref, kg_ref, vg_ref, bd_ref, wcat_ref, sbias_ref,
                    seq_local):
    tm = x.shape[0]
    ms = jnp.mean(x * x, axis=-1, keepdims=True)
    h = ((x * lax.rsqrt(ms + EPS)) * g_ref[...]).astype(BF16)

    def seg(lo, hi):
        w_head, w_tail, w_forget = w_ref
        if hi <= C_U:
            return _dot(h, w_head[:, lo:hi])
        if lo >= C_F:
            return _dot(h, w_forget[...])
        return _dot(h, w_tail[:, lo - C_U:hi - C_U])

    def head_norm(z, gain):
        msz = _dot((z * z).astype(BF16), bd_ref[...]) * (1.0 / HEAD_DIM)
        return z * lax.rsqrt(msz + EPS) * gain

    if bd_ref is None:
        qn, kn = seg(C_Q, C_K), seg(C_K, C_V)
    else:
        qn = head_norm(seg(C_Q, C_K), qg_ref[...])
        kn = head_norm(seg(C_K, C_V), kg_ref[...])
    v = seg(C_V, C_GA)
    zga = seg(C_GA, C_U)
    ga = zga * _sigmoid(zga)

    lane = lax.broadcasted_iota(jnp.int32, (tm, LANES), 1)
    logf = jnp.where(lane < N_HEADS, _log_sigmoid(seg(C_F, C_END) + bf_ref[...]), 0.0)

    u = _gelu_tanh(seg(C_U, C_VB))
    vbg = _gelu_tanh(seg(C_VB, C_GB))
    msv = jnp.mean(vbg * vbg, axis=-1, keepdims=True)
    vbn = vbg * lax.rsqrt(msv + EPS) * vg_ref[...]
    zgb = seg(C_GB, C_SA)
    ugb = u * (zgb * _sigmoid(zgb))

    t_idx = lax.broadcasted_iota(jnp.int32, (CHUNK, N_GROUPS * CHUNK), 0)
    s_idx = lax.broadcasted_iota(jnp.int32, (CHUNK, N_GROUPS * CHUNK), 1) & (CHUNK - 1)
    keep = s_idx <= t_idx
    if seq_local < CHUNK:
        shift = int(math.log2(seq_local))
        keep = keep & ((s_idx >> shift) == (t_idx >> shift))
    wm = jnp.where(keep, wcat_ref[...], jnp.zeros((), BF16))
    gq = N_GROUPS // 2
    glane = lax.broadcasted_iota(jnp.int32, (CHUNK, gq * GROUP_DIM), 1) >> GROUP_DIM_BITS
    vbn_b = vbn.astype(BF16)
    s_chunks = []
    for c in range(tm // CHUNK):
        halves = []
        for hf in range(2):
            vc = vbn_b[c * CHUNK:(c + 1) * CHUNK, hf * gq * GROUP_DIM:(hf + 1) * gq * GROUP_DIM]
            rhs = jnp.concatenate(
                [jnp.where(glane == g, vc, jnp.zeros((), BF16)) for g in range(gq)], axis=0)
            halves.append(_dot(wm[:, hf * gq * CHUNK:(hf + 1) * gq * CHUNK], rhs))
        s_chunks.append(jnp.concatenate(halves, axis=1) + sbias_ref[...])
    s_sgu = jnp.concatenate(s_chunks, axis=0)
    bmix = ugb * s_sgu

    sa = _sigmoid(seg(C_SA, C_SB))
    sb = _sigmoid(seg(C_SB, C_F))
    return qn, kn, v, logf, vbn, ga, bmix, sa, sb


def _in_proj_prompt_kernel(n_alias, *refs):
    (x_ref, g_ref, w_ref, bf_ref, qgt_ref, kgt_ref, vg_ref, wcat_ref, sbias_ref, tri_ref, pk_ref,
     pq_ref, qt_ref, ka_ref, vt_ref, ko_ref, vo_ref, lf_ref, ga_ref, bm_ref, sa_ref, sb_ref,
     carry_ref) = refs[n_alias:]
    sr = vt_ref.shape[-1]

    @pl.when(pl.program_id(1) == 0)
    def _():
        carry_ref[...] = jnp.zeros_like(carry_ref)

    def head_norm_t(zt, gain_t):
        sq = zt * zt
        parts = []
        for hh in range(2):
            ms = jnp.sum(sq[hh * HEAD_DIM:(hh + 1) * HEAD_DIM], axis=0, keepdims=True) * (1.0 / HEAD_DIM)
            parts.append(jnp.broadcast_to(lax.rsqrt(ms + EPS), (HEAD_DIM, sr)))
        return zt * jnp.concatenate(parts, axis=0) * gain_t

    carry = carry_ref[...]
    lane = lax.broadcasted_iota(jnp.int32, (sr, LANES), 1)
    row8 = lax.broadcasted_iota(jnp.int32, (N_HEADS, sr), 0)
    pad_rows = jnp.zeros((LANES - N_HEADS, sr), F32)
    ones_blk = jnp.where(lax.broadcasted_iota(jnp.int32, (VT_ROWS - HEAD_DIM, sr), 0) == 0, 1.0, 0.0)
    for sub in range(x_ref.shape[-2] // sr):
        rows = slice(sub * sr, (sub + 1) * sr)
        zq, zk, v, logf, _, ga, bmix, sa, sb = _in_proj_common(
            x_ref[0, rows], g_ref, w_ref, bf_ref, None, None, vg_ref, None, wcat_ref, sbias_ref, CHUNK)

        lft = logf.T[0:N_HEADS]
        lf_ref[0, 0, :, rows] = lft
        ga_ref[0, rows] = ga.astype(BF16)
        bm_ref[0, rows] = bmix.astype(BF16)
        sa_ref[0, rows] = sa.astype(BF16)
        sb_ref[0, rows] = sb.astype(BF16)

        stacked = jnp.concatenate([part.astype(F32) for part in _split3(lft)]
                                  + [jnp.zeros((N_HEADS, sr), F32)], axis=0).astype(BF16)
        cs = _dot(stacked, tri_ref[...])
        ct = cs[0:N_HEADS] + cs[N_HEADS:2 * N_HEADS] + cs[2 * N_HEADS:3 * N_HEADS] + carry
        carry = jnp.broadcast_to(ct[:, sr - 1:sr], (N_HEADS, sr))
        c2t = ct * LOG2E
        c2 = jnp.concatenate([c2t, pad_rows], axis=0).T

        hi, mid, lo = _split3(c2)
        cp = (hi.astype(F32) + pltpu.roll(mid.astype(F32), 8, axis=1)
              + pltpu.roll(lo.astype(F32), 16, axis=1) + jnp.where(lane == 24, 1.0, 0.0))
        extra_k = _dot(cp.astype(BF16), pk_ref[...])

        hit, midt, lot = (part.astype(F32) for part in _split3(c2t))
        cpt = jnp.concatenate(
            [hit, midt, lot, jnp.where(row8 == 0, 1.0, 0.0), jnp.zeros((LANES - 32, sr), F32)],
            axis=0).astype(BF16)
        extra_q = _dot(pq_ref[...], cpt)

        for j in range(N_HEADS // 2):
            qs_t = head_norm_t(zq[:, j * LANES:(j + 1) * LANES].T, qgt_ref[...])
            ks_t = head_norm_t(zk[:, j * LANES:(j + 1) * LANES].T, kgt_ref[...])
            vs_t = v[:, j * LANES:(j + 1) * LANES].T
            ks = ks_t.T
            ks_sw = pltpu.roll(ks, HEAD_DIM, axis=1)
            ko_ref[0, 0, j * LANES:(j + 1) * LANES, rows] = ks_t
            vo_ref[0, 0, j * LANES:(j + 1) * LANES, rows] = vs_t
            for hh in range(2):
                hd = 2 * j + hh
                qt_ref[0, hd, :, rows] = jnp.concatenate(
                    [qs_t[hh * HEAD_DIM:(hh + 1) * HEAD_DIM], extra_q[hd * HEAD_DIM:(hd + 1) * HEAD_DIM]],
                    axis=0).astype(BF16)
                vt_ref[0, hd, sub] = jnp.concatenate(
                    [vs_t[hh * HEAD_DIM:(hh + 1) * HEAD_DIM], ones_blk], axis=0).astype(BF16)
                ka_ref[0, hd, rows] = jnp.where(lane < HEAD_DIM, ks if hh == 0 else ks_sw,
                                                extra_k).astype(BF16)
    carry_ref[...] = carry


def _in_proj_sample_kernel(x_ref, g_ref, w_ref, bf_ref, qg_ref, kg_ref, vg_ref, bd_ref, wcat_ref,
                           sbias_ref,
                           q_ref, ko_ref, vo_ref, lf_ref, lfp_ref, vbn_ref, ga_ref, bm_ref, sa_ref, sb_ref,
                           *, seq_local):
    qn, kn, v, logf, vbn, ga, bmix, sa, sb = _in_proj_common(
        x_ref[...], g_ref, w_ref, bf_ref, qg_ref, kg_ref, vg_ref, bd_ref, wcat_ref, sbias_ref, seq_local)
    q_ref[...] = qn * (HEAD_DIM ** -0.5 * LOG2E)
    ko_ref[...] = kn
    vo_ref[...] = v
    lf_ref[...] = logf[:, :N_HEADS]
    lfp_ref[...] = logf
    vbn_ref[...] = vbn
    ga_ref[...] = ga.astype(BF16)
    bm_ref[...] = bmix.astype(BF16)
    sa_ref[...] = sa.astype(BF16)
    sb_ref[...] = sb.astype(BF16)


def _const_spec(arr):
    if isinstance(arr, tuple):
        return tuple(_const_spec(a) for a in arr)
    return pl.BlockSpec(arr.shape, lambda *_: (0,) * arr.ndim, pipeline_mode=pl.Buffered(1))


def _in_proj_consts(tm):
    bd = np.kron(np.eye(N_HEADS, dtype=np.float32), np.ones((HEAD_DIM, HEAD_DIM), np.float32))
    tri = np.triu(np.ones((tm, tm), np.float32))
    pk = np.zeros((LANES, LANES), np.float32)
    pq = np.zeros((N_HEADS * HEAD_DIM, LANES), np.float32)
    for hd in range(N_HEADS):
        for part in range(3):
            pk[8 * part + hd, HEAD_DIM + 3 * hd + part] = -1.0
            pk[24, HEAD_DIM + 3 * N_HEADS + part] = 1.0
            pq[hd * HEAD_DIM + 3 * hd + part, 24] = 1.0
            pq[hd * HEAD_DIM + 3 * N_HEADS + part, 8 * part + hd] = 1.0
    return (jnp.asarray(bd, BF16), jnp.asarray(tri, BF16), jnp.asarray(pk, BF16), jnp.asarray(pq, BF16))


def _in_proj_prompt(x, params, tm, slab, layer, depth, stacked):
    b, s, _ = x.shape
    nt = s // tm
    _, tri, pk, pq = _in_proj_consts(slab)
    g, w, bf, qgt, kgt, vg, wcat, sbias = params
    consts = (g, w, bf, qgt, kgt, vg, wcat, sbias, tri, pk, pq)
    stacked = () if stacked is None else tuple(stacked)
    row3 = lambda width: pl.BlockSpec((1, tm, width), lambda bi, i: (bi, i, 0))
    col4 = lambda rows: pl.BlockSpec((1, 1, rows, tm), lambda bi, i: (layer, bi, 0, i))
    out_shape = (
        jax.ShapeDtypeStruct((b, N_HEADS, LANES, s), BF16),
        jax.ShapeDtypeStruct((b, N_HEADS, s, LANES), BF16),
        jax.ShapeDtypeStruct((b, N_HEADS, s // slab, VT_ROWS, slab), BF16),
        jax.ShapeDtypeStruct((depth, b, WIDTH_A, s), F32),
        jax.ShapeDtypeStruct((depth, b, WIDTH_A, s), F32),
        jax.ShapeDtypeStruct((depth, b, N_HEADS, s), F32),
        jax.ShapeDtypeStruct((b, s, WIDTH_A), BF16),
        jax.ShapeDtypeStruct((b, s, WIDTH_B), BF16),
        jax.ShapeDtypeStruct((b, s, D_MODEL), BF16),
        jax.ShapeDtypeStruct((b, s, D_MODEL), BF16),
    )
    out_specs = (
        pl.BlockSpec((1, N_HEADS, LANES, tm), lambda bi, i: (bi, 0, 0, i)),
        pl.BlockSpec((1, N_HEADS, tm, LANES), lambda bi, i: (bi, 0, i, 0)),
        pl.BlockSpec((1, N_HEADS, tm // slab, VT_ROWS, slab), lambda bi, i: (bi, 0, i, 0, 0)),
        col4(WIDTH_A), col4(WIDTH_A), col4(N_HEADS), row3(WIDTH_A), row3(WIDTH_B),
        row3(D_MODEL), row3(D_MODEL),
    )
    return pl.pallas_call(
        functools.partial(_in_proj_prompt_kernel, len(stacked)),
        out_shape=out_shape,
        grid=(b, nt),
        in_specs=[pl.BlockSpec(memory_space=pl.ANY)] * len(stacked) + [row3(D_MODEL)]
        + [_const_spec(c) for c in consts],
        out_specs=out_specs,
        scratch_shapes=[pltpu.VMEM((N_HEADS, slab), F32)],
        input_output_aliases={n: 3 + n for n in range(len(stacked))},
        compiler_params=pltpu.CompilerParams(
            dimension_semantics=("arbitrary", "arbitrary"), vmem_limit_bytes=VMEM_LIMIT),
        name="in_proj_prompt",
    )(*stacked, x, *consts)


def _in_proj_sample(x, params, tm, seq_local):
    n = x.shape[0]
    bd = _in_proj_consts(tm)[0]
    g, w, bf, qg, kg, vg, wcat, sbias = params
    consts = (g, w, bf, qg, kg, vg, bd, wcat, sbias)
    row2 = lambda width: pl.BlockSpec((tm, width), lambda i: (i, 0))
    out_shape = (
        jax.ShapeDtypeStruct((n, WIDTH_A), F32),
        jax.ShapeDtypeStruct((n, WIDTH_A), F32),
        jax.ShapeDtypeStruct((n, WIDTH_A), F32),
        jax.ShapeDtypeStruct((n, N_HEADS), F32),
        jax.ShapeDtypeStruct((n, LANES), F32),
        jax.ShapeDtypeStruct((n, WIDTH_B), F32),
        jax.ShapeDtypeStruct((n, WIDTH_A), BF16),
        jax.ShapeDtypeStruct((n, WIDTH_B), BF16),
        jax.ShapeDtypeStruct((n, D_MODEL), BF16),
        jax.ShapeDtypeStruct((n, D_MODEL), BF16),
    )
    out_specs = (row2(WIDTH_A), row2(WIDTH_A), row2(WIDTH_A), row2(N_HEADS), row2(LANES), row2(WIDTH_B),
                 row2(WIDTH_A), row2(WIDTH_B), row2(D_MODEL), row2(D_MODEL))
    return pl.pallas_call(
        functools.partial(_in_proj_sample_kernel, seq_local=seq_local),
        out_shape=out_shape,
        grid=(n // tm,),
        in_specs=[row2(D_MODEL)] + [_const_spec(c) for c in consts],
        out_specs=out_specs,
        compiler_params=pltpu.CompilerParams(
            dimension_semantics=("arbitrary",), vmem_limit_bytes=VMEM_LIMIT),
        name="in_proj_sample",
    )(x, *consts)


def _attn_prompt_kernel(qt_ref, ka_ref, vt_ref, o_ref, s_ref, m_ref, al_ref, acc_ref):
    g = qt_ref.shape[1]
    tq = qt_ref.shape[-1]
    t = vt_ref.shape[-1]
    assert tq == 2 * t
    i = pl.program_id(2)
    key = lax.broadcasted_iota(jnp.int32, (t, tq), 0)
    qry = lax.broadcasted_iota(jnp.int32, (t, tq), 1)

    def scores(hd, blk, slot, mask, m_old):
        k = ka_ref[0, hd, pl.ds(pl.multiple_of(blk * t, t), t), :]
        s = _dot(k, qt_ref[0, hd])
        if mask is not None:
            s = jnp.where(mask, s, NEG_BIG)
        s_ref[slot, hd] = s
        m_new = jnp.maximum(m_old, jnp.max(s, axis=0, keepdims=True))
        m_ref[slot, hd, 0:1, :] = m_new
        al_ref[slot, hd, 0:1, :] = jnp.exp2(m_old - m_new)

    def iteration(j, cur, next_mask, has_next):
        for hd in range(g):
            p = jnp.exp2(s_ref[cur, hd] - m_ref[cur, hd, 0:1, :]).astype(BF16)
            if has_next:
                scores(hd, j + 1, 1 - cur, next_mask, m_ref[cur, hd, 0:1, :])
            acc_ref[hd] = al_ref[cur, hd, 0:1, :] * acc_ref[hd] + _dot(vt_ref[0, hd, j], p)

    acc_ref[...] = jnp.zeros_like(acc_ref)
    diag_lo = key <= qry
    diag_hi = key + t <= qry
    for hd in range(g):
        scores(hd, 0, 0, key <= qry + jnp.where(i > 0, tq, 0), jnp.full((1, tq), NEG_BIG, F32))

    def pair(jj, carry):
        iteration(2 * jj, 0, None, True)
        iteration(2 * jj + 1, 1, None, True)
        return carry

    lax.fori_loop(0, i - 1, pair, 0)

    @pl.when(i >= 1)
    def _():
        iteration(2 * i - 2, 0, None, True)
        iteration(2 * i - 1, 1, diag_lo, True)

    last = 2 * i + 1
    iteration(last - 1, 0, diag_hi, True)
    iteration(last, 1, None, False)
    for hd in range(g):
        acc = acc_ref[hd]
        o_ref[0, hd * HEAD_DIM:(hd + 1) * HEAD_DIM, :] = acc[0:HEAD_DIM] / acc[HEAD_DIM:HEAD_DIM + 1]


def _attn_prompt(qt, ka, vt, g):
    b, h, _, s = qt.shape
    t = vt.shape[-1]
    tq = 2 * t
    return pl.pallas_call(
        _attn_prompt_kernel,
        out_shape=jax.ShapeDtypeStruct((b, WIDTH_A, s), F32),
        grid=(b, h // g, s // tq),
        in_specs=[
            pl.BlockSpec((1, g, LANES, tq), lambda bi, hi, i: (bi, hi, 0, i)),
            pl.BlockSpec((1, g, s, LANES), lambda bi, hi, i: (bi, hi, 0, 0),
                         pipeline_mode=pl.Buffered(1)),
            pl.BlockSpec((1, g, s // t, VT_ROWS, t), lambda bi, hi, i: (bi, hi, 0, 0, 0),
                         pipeline_mode=pl.Buffered(1)),
        ],
        out_specs=pl.BlockSpec((1, g * HEAD_DIM, tq), lambda bi, hi, i: (bi, hi, i)),
        scratch_shapes=[pltpu.VMEM((2, g, t, tq), F32),
                        pltpu.VMEM((2, g, 8, tq), F32), pltpu.VMEM((2, g, 8, tq), F32),
                        pltpu.VMEM((g, VT_ROWS, tq), F32)],
        compiler_params=pltpu.CompilerParams(
            dimension_semantics=("arbitrary", "arbitrary", "arbitrary"), vmem_limit_bytes=VMEM_LIMIT),
        name="attn_prompt",
    )(qt, ka, vt)


def _attn_sample_kernel(n_pages, t_new, seqs, pt_ref, q_ref, kn_ref, vn_ref, ln_ref, ls_ref, ones_ref, m3_ref,
                        *rest):
    del pt_ref
    n_all = seqs * n_pages
    o_ref = rest[3 * n_all]
    for u in range(seqs):
        tok = slice(u * t_new, (u + 1) * t_new)
        pages = [rest[grp * n_all + u * n_pages:grp * n_all + (u + 1) * n_pages] for grp in range(3)]
        o_ref[tok] = _attn_sample_one(q_ref[tok], kn_ref[tok], vn_ref[tok], ln_ref[tok],
                                      ls_ref, ones_ref, m3_ref, *pages)


def _attn_sample_one(q, k_new, v_new, logf_new, ls_ref, ones_ref, m3_ref, k_refs, v_refs, l_refs):
    t_new = q.shape[0]
    n_pages = len(k_refs)
    rows = t_new * N_HEADS

    def per_token(x):
        return jnp.concatenate(
            [jnp.broadcast_to(x[tk:tk + 1], (N_HEADS, x.shape[1])) for tk in range(t_new)], axis=0)

    row = lax.broadcasted_iota(jnp.int32, (rows, WIDTH_A), 0)
    lane = lax.broadcasted_iota(jnp.int32, (rows, WIDTH_A), 1)
    own_head = (row & (N_HEADS - 1)) == (lane >> HEAD_DIM_BITS)
    qbd = jnp.where(own_head, per_token(q), 0.0).astype(BF16)

    cn = logf_new * LOG2E
    trow = lax.broadcasted_iota(jnp.int32, (t_new, LANES), 0)
    sh = 1
    while sh < t_new:
        cn = cn + jnp.where(trow >= sh, pltpu.roll(cn, sh, axis=0), 0.0)
        sh *= 2
    row1 = lax.broadcasted_iota(jnp.int32, (rows, LANES), 0)
    lane1 = lax.broadcasted_iota(jnp.int32, (rows, LANES), 1)
    head_lane = (row1 & (N_HEADS - 1)) == lane1
    cn_col = jnp.sum(jnp.where(head_lane, per_token(cn), 0.0), axis=1, keepdims=True)

    x = jnp.concatenate([l_refs[p][0, 0] for p in range(n_pages)], axis=0) * LOG2E
    xp = _split3(x)
    r_in = _dot3_right(xp, ls_ref[...])
    tot_b = _dot3_right(xp, ones_ref[...])
    r_page = _dot3_left(m3_ref[...], _split3(tot_b))
    r = r_in + r_page

    s_blocks = []
    for p in range(n_pages):
        kp_t = k_refs[p][0, 0].astype(BF16)
        r_p = r[p * N_HEADS:(p + 1) * N_HEADS]
        s_blocks.append(_dot(qbd, kp_t) + jnp.concatenate([r_p] * t_new, axis=0) + cn_col)
    pad = jnp.zeros((PAGE - t_new, WIDTH_A), F32)
    kn = jnp.concatenate([k_new, pad], axis=0).astype(BF16)
    vn = jnp.concatenate([v_new, pad], axis=0).astype(BF16)
    cn_pad = jnp.concatenate([cn, jnp.zeros((PAGE - t_new, LANES), F32)], axis=0)
    sel = jnp.where(head_lane, 1.0, 0.0).astype(BF16)
    cn_keys = sum(_dot_nt(sel, part) for part in _split3(cn_pad))
    s_new = _dot_nt(qbd, kn) + cn_col - cn_keys
    s_new = jnp.where((lane1 <= (row1 >> HEAD_BITS)) & (lane1 < t_new), s_new, NEG_BIG)
    s_blocks.append(s_new)

    m = s_blocks[0]
    for sb in s_blocks[1:]:
        m = jnp.maximum(m, sb)
    m = jnp.max(m, axis=1, keepdims=True)
    l_sum = jnp.zeros((rows, 1), F32)
    o = jnp.zeros((rows, WIDTH_A), F32)
    for p in range(n_pages + 1):
        pe = jnp.exp2(s_blocks[p] - m)
        l_sum = l_sum + jnp.sum(pe, axis=1, keepdims=True)
        if p < n_pages:
            o = o + _dot_nt(pe.astype(BF16), v_refs[p][0, 0].astype(BF16))
        else:
            o = o + _dot(pe.astype(BF16), vn)
    o = jnp.where(own_head, o / l_sum, 0.0)
    return jnp.concatenate(
        [jnp.sum(o[tk * N_HEADS:(tk + 1) * N_HEADS], axis=0, keepdims=True) for tk in range(t_new)], axis=0)


def _attn_sample(layer, page_table, q, kn, vn, lnp, ck, cv, clt, t_new, seqs):
    n_seq, n_pages = page_table.shape
    pr = n_pages * N_HEADS
    key = np.arange(PAGE)
    ls = (key[:, None] > key[None, :]).astype(np.float32)
    ones = np.ones((PAGE, LANES), np.float32)
    ph = np.arange(pr)
    m3 = ((ph[:, None] % N_HEADS == ph[None, :] % N_HEADS)
          & (ph[None, :] // N_HEADS > ph[:, None] // N_HEADS)).astype(np.float32)
    consts = tuple(jnp.asarray(a, BF16) for a in (ls, ones, m3))

    tok = lambda width: pl.BlockSpec((seqs * t_new, width), lambda b, pt: (b, 0))
    def page_specs(block):
        return [pl.BlockSpec(block, functools.partial(
            lambda b, pt, u, p: (layer, pt[b * seqs + u, p], 0, 0), u=u, p=p))
                for u in range(seqs) for p in range(n_pages)]
    grid_spec = pltpu.PrefetchScalarGridSpec(
        num_scalar_prefetch=1,
        grid=(n_seq // seqs,),
        in_specs=[tok(WIDTH_A), tok(WIDTH_A), tok(WIDTH_A), tok(LANES)]
        + [pl.BlockSpec(c.shape, lambda b, pt: (0, 0)) for c in consts]
        + page_specs((1, 1, WIDTH_A, PAGE)) + page_specs((1, 1, WIDTH_A, PAGE))
        + page_specs((1, 1, N_HEADS, PAGE)),
        out_specs=tok(WIDTH_A),
    )
    return pl.pallas_call(
        functools.partial(_attn_sample_kernel, n_pages, t_new, seqs),
        out_shape=jax.ShapeDtypeStruct((n_seq * t_new, WIDTH_A), F32),
        grid_spec=grid_spec,
        compiler_params=pltpu.CompilerParams(
            dimension_semantics=("arbitrary",), vmem_limit_bytes=VMEM_LIMIT),
        name="attn_sample",
    )(page_table, q, kn, vn, lnp, *consts,
      *([ck] * (seqs * n_pages)), *([cv] * (seqs * n_pages)), *([clt] * (seqs * n_pages)))


def _merge_kernel(transposed, slab, a_ref, ga_ref, bm_ref, sa_ref, sb_ref, x_ref, wpa_ref, wpb_ref, wo_ref,
                  y_ref):
    for sub in range(x_ref.shape[-2] // slab):
        rows = slice(sub * slab, (sub + 1) * slab)
        at = (lambda ref: ref[0, rows]) if transposed else (lambda ref: ref[rows])
        attn = a_ref[0, :, rows].T if transposed else a_ref[rows]
        a = (attn * at(ga_ref).astype(F32)).astype(BF16)
        pa = _dot(a, wpa_ref[...])
        pb = _dot(at(bm_ref), wpb_ref[...])
        m = at(sa_ref).astype(F32) * pa + at(sb_ref).astype(F32) * pb
        y = at(x_ref) + _dot(m.astype(BF16), wo_ref[...])
        if transposed:
            y_ref[0, rows] = y
        else:
            y_ref[rows] = y


def _merge_prompt(attn_t, ga, bm, sa, sb, x, wpa, wpb, wo, tm):
    b, s, _ = x.shape
    row3 = lambda width: pl.BlockSpec((1, tm, width), lambda bi, i: (bi, i, 0))
    return pl.pallas_call(
        functools.partial(_merge_kernel, True, MERGE_SLAB),
        out_shape=jax.ShapeDtypeStruct((b, s, D_MODEL), F32),
        grid=(b, s // tm),
        in_specs=[pl.BlockSpec((1, WIDTH_A, tm), lambda bi, i: (bi, 0, i)),
                  row3(WIDTH_A), row3(WIDTH_B), row3(D_MODEL), row3(D_MODEL), row3(D_MODEL),
                  _const_spec(wpa), _const_spec(wpb), _const_spec(wo)],
        out_specs=row3(D_MODEL),
        compiler_params=pltpu.CompilerParams(
            dimension_semantics=("arbitrary", "arbitrary"), vmem_limit_bytes=VMEM_LIMIT),
        name="merge_prompt",
    )(attn_t, ga, bm, sa, sb, x, wpa, wpb, wo)


def _merge_sample(attn, ga, bm, sa, sb, x, wpa, wpb, wo, tm):
    n = x.shape[0]
    row2 = lambda width: pl.BlockSpec((tm, width), lambda i: (i, 0))
    return pl.pallas_call(
        functools.partial(_merge_kernel, False, tm),
        out_shape=jax.ShapeDtypeStruct((n, D_MODEL), F32),
        grid=(n // tm,),
        in_specs=[row2(WIDTH_A), row2(WIDTH_A), row2(WIDTH_B), row2(D_MODEL), row2(D_MODEL), row2(D_MODEL),
                  _const_spec(wpa), _const_spec(wpb), _const_spec(wo)],
        out_specs=row2(D_MODEL),
        compiler_params=pltpu.CompilerParams(
            dimension_semantics=("arbitrary",), vmem_limit_bytes=VMEM_LIMIT),
        name="merge_sample",
    )(attn, ga, bm, sa, sb, x, wpa, wpb, wo)


KEY_BLOCK = 256
TM_PROMPT = 512
HEADS_PER_STEP = 8
SEQS_PER_STEP = 2
TM_MERGE = 1024
MERGE_SLAB = 512
TM_SAMPLE = 256


def kernel(x_prompt, x_sample, cache_k, cache_v, cache_logf, page_table, norm_g, w_in, b_forget, q_norm_g,
           k_norm_g, v_norm_g, w_s, b_s, w_proj_a, w_proj_b, w_out):
    depth = w_in.shape[0]
    bsz, seq, _ = x_prompt.shape
    n_seq, t_new, _ = x_sample.shape
    n_pool = cache_k.shape[1]

    f_lo = 4 * WIDTH_A
    w_head = w_in[:, :, :f_lo].astype(BF16)
    w_tail = w_in[:, :, f_lo + N_HEADS:].astype(BF16)
    w_forget = jnp.pad(w_in[:, :, f_lo:f_lo + N_HEADS],
                       ((0, 0), (0, 0), (0, C_END - C_F - N_HEADS))).astype(BF16)
    bf_pad = jnp.pad(b_forget, ((0, 0), (0, LANES - N_HEADS)))[:, None, :]
    qg = jnp.tile(q_norm_g, (1, N_HEADS))[:, None, :]
    kg = jnp.tile(k_norm_g, (1, N_HEADS))[:, None, :]
    gain_cols = lambda gn: jnp.tile(gn[:, :, None], (1, 2, KEY_BLOCK))
    qg_t = gain_cols(q_norm_g * (HEAD_DIM ** -0.5 * LOG2E))
    kg_t = gain_cols(k_norm_g)
    vg = v_norm_g[:, None, :]
    ng = norm_g[:, None, :]
    wcat_p = w_s.transpose(0, 2, 1, 3).reshape(depth, CHUNK, N_GROUPS * CHUNK).astype(BF16)
    w_small = jnp.tile(w_s[:, :, :t_new, :t_new], (1, 1, CHUNK // t_new, CHUNK // t_new))
    wcat_s = w_small.transpose(0, 2, 1, 3).reshape(depth, CHUNK, N_GROUPS * CHUNK).astype(BF16)
    sbias_p = jnp.repeat(b_s.transpose(0, 2, 1), GROUP_DIM, axis=-1)
    sbias_s = jnp.repeat(jnp.tile(b_s[:, :, :t_new], (1, 1, CHUNK // t_new)).transpose(0, 2, 1),
                         GROUP_DIM, axis=-1)
    wpa = w_proj_a.astype(BF16)
    wpb = w_proj_b.astype(BF16)
    wo = w_out.astype(BF16)
    ck = cache_k.transpose(0, 1, 3, 4, 2).reshape(depth, n_pool, WIDTH_A, PAGE)
    cv = cache_v.transpose(0, 1, 3, 4, 2).reshape(depth, n_pool, WIDTH_A, PAGE)
    clt = cache_logf.transpose(0, 1, 3, 2)

    xp = x_prompt
    xs = x_sample.reshape(n_seq * t_new, D_MODEL)
    outs = [[] for _ in range(4)]
    stacked = None
    for l in range(depth):
        weights = (w_head[l], w_tail[l], w_forget[l])
        qt, ka, vt, *stacked, ga, bm, sa, sb = _in_proj_prompt(
            xp, (ng[l], weights, bf_pad[l], qg_t[l], kg_t[l], vg[l], wcat_p[l], sbias_p[l]),
            TM_PROMPT, KEY_BLOCK, l, depth, stacked)
        common = (ng[l], weights, bf_pad[l], qg[l], kg[l], vg[l])
        attn_t = _attn_prompt(qt, ka, vt, HEADS_PER_STEP)
        xp = _merge_prompt(attn_t, ga, bm, sa, sb, xp, wpa[l], wpb[l], wo[l], TM_MERGE)

        q_s, k_s, v_s, lf_s, lfp_s, vbn_s, ga, bm, sa, sb = _in_proj_sample(
            xs, common + (wcat_s[l], sbias_s[l]), TM_SAMPLE, t_new)
        attn_s = _attn_sample(l, page_table, q_s, k_s, v_s, lfp_s, ck, cv, clt, t_new, SEQS_PER_STEP)
        xs = _merge_sample(attn_s, ga, bm, sa, sb, xs, wpa[l], wpb[l], wo[l], TM_SAMPLE)
        outs[0].append(k_s.reshape(n_seq, t_new, N_HEADS, HEAD_DIM))
        outs[1].append(v_s.reshape(n_seq, t_new, N_HEADS, HEAD_DIM))
        outs[2].append(lf_s.reshape(n_seq, t_new, N_HEADS))
        outs[3].append(vbn_s.reshape(n_seq, t_new, WIDTH_B))
    kt_all, vt_all, lft_all = stacked
    unstack = lambda a: a.reshape(depth, bsz, N_HEADS, HEAD_DIM, seq).transpose(0, 1, 4, 2, 3)
    return ((xp, xs.reshape(n_seq, t_new, D_MODEL), unstack(kt_all), unstack(vt_all),
             lft_all.transpose(0, 1, 3, 2)) + tuple(jnp.stack(o) for o in outs))
```

```python
import functools
import math

import numpy as np
import jax
import jax.numpy as jnp
from jax import lax
from jax.experimental import pallas as pl
from jax.experimental.pallas import tpu as pltpu

D_MODEL = 1024
N_HEADS = 8
HEAD_DIM = 64
HEAD_BITS = 3
HEAD_DIM_BITS = 6
WIDTH_A = N_HEADS * HEAD_DIM
N_GROUPS = 8
GROUP_DIM = 64
GROUP_DIM_BITS = 6
WIDTH_B = N_GROUPS * GROUP_DIM
CHUNK = 128
PAGE = 128
EPS = 1e-6
LOG2E = math.log2(math.e)
NEG_BIG = -1e30

LANES = 128
VT_ROWS = 80
VMEM_LIMIT = 56 * 1024 * 1024

C_Q, C_K, C_V, C_GA, C_U, C_VB, C_GB, C_SA, C_SB, C_F, C_END = (
    0, 512, 1024, 1536, 2048, 2560, 3072, 3584, 4608, 5632, 5760)

F32 = jnp.float32
BF16 = jnp.bfloat16


def _split3(x):
    hi = x.astype(BF16)
    r1 = x - hi.astype(F32)
    mid = r1.astype(BF16)
    lo = (r1 - mid.astype(F32)).astype(BF16)
    return hi, mid, lo


def _dot(a, b):
    return jnp.dot(a, b, preferred_element_type=F32)


def _dot_nt(a, b):
    return lax.dot_general(a, b, (((1,), (1,)), ((), ())), preferred_element_type=F32)


def _dot3_right(parts, w):
    return _dot(parts[0], w) + _dot(parts[1], w) + _dot(parts[2], w)


def _dot3_left(w, parts):
    return _dot(w, parts[0]) + _dot(w, parts[1]) + _dot(w, parts[2])


def _sigmoid(x):
    return 1.0 / (1.0 + jnp.exp(-x))


def _gelu_tanh(x):
    c = math.sqrt(2.0 / math.pi)
    return 0.5 * x * (1.0 + jnp.tanh(c * (x + 0.044715 * (x * x * x))))


def _log_sigmoid(x):
    return jnp.minimum(x, 0.0) - jnp.log1p(jnp.exp(-jnp.abs(x)))


def _in_proj_common(x, g_ref, w_ref, bf_ref, qg_ref, kg_ref, vg_ref, bd_ref, wcat_ref, sbias_ref,
                    seq_local):
    tm = x.shape[0]
    ms = jnp.mean(x * x, axis=-1, keepdims=True)
    h = ((x * lax.rsqrt(ms + EPS)) * g_ref[...]).astype(BF16)

    def seg(lo, hi):
        w_head, w_tail, w_forget = w_ref
        if hi <= C_U:
            return _dot(h, w_head[:, lo:hi])
        if lo >= C_F:
            return _dot(h, w_forget[...])
        return _dot(h, w_tail[:, lo - C_U:hi - C_U])

    def head_norm(z, gain):
        msz = _dot((z * z).astype(BF16), bd_ref[...]) * (1.0 / HEAD_DIM)
        return z * lax.rsqrt(msz + EPS) * gain

    if bd_ref is None:
        qn, kn = seg(C_Q, C_K), seg(C_K, C_V)
    else:
        qn = head_norm(seg(C_Q, C_K), qg_ref[...])
        kn = head_norm(seg(C_K, C_V), kg_ref[...])
    v = seg(C_V, C_GA)
    zga = seg(C_GA, C_U)
    ga = zga * _sigmoid(zga)

    lane = lax.broadcasted_iota(jnp.int32, (tm, LANES), 1)
    logf = jnp.where(lane < N_HEADS, _log_sigmoid(seg(C_F, C_END) + bf_ref[...]), 0.0)

    u = _gelu_tanh(seg(C_U, C_VB))
    vbg = _gelu_tanh(seg(C_VB, C_GB))
    msv = jnp.mean(vbg * vbg, axis=-1, keepdims=True)
    vbn = vbg * lax.rsqrt(msv + EPS) * vg_ref[...]
    zgb = seg(C_GB, C_SA)
    ugb = u * (zgb * _sigmoid(zgb))

    t_idx = lax.broadcasted_iota(jnp.int32, (CHUNK, N_GROUPS * CHUNK), 0)
    s_idx = lax.broadcasted_iota(jnp.int32, (CHUNK, N_GROUPS * CHUNK), 1) & (CHUNK - 1)
    keep = s_idx <= t_idx
    if seq_local < CHUNK:
        shift = int(math.log2(seq_local))
        keep = keep & ((s_idx >> shift) == (t_idx >> shift))
    wm = jnp.where(keep, wcat_ref[...], jnp.zeros((), BF16))
    gq = N_GROUPS // 2
    glane = lax.broadcasted_iota(jnp.int32, (CHUNK, gq * GROUP_DIM), 1) >> GROUP_DIM_BITS
    vbn_b = vbn.astype(BF16)
    s_chunks = []
    for c in range(tm // CHUNK):
        halves = []
        for hf in range(2):
            vc = vbn_b[c * CHUNK:(c + 1) * CHUNK, hf * gq * GROUP_DIM:(hf + 1) * gq * GROUP_DIM]
            rhs = jnp.concatenate(
                [jnp.where(glane == g, vc, jnp.zeros((), BF16)) for g in range(gq)], axis=0)
            halves.append(_dot(wm[:, hf * gq * CHUNK:(hf + 1) * gq * CHUNK], rhs))
        s_chunks.append(jnp.concatenate(halves, axis=1) + sbias_ref[...])
    s_sgu = jnp.concatenate(s_chunks, axis=0)
    bmix = ugb * s_sgu

    sa = _sigmoid(seg(C_SA, C_SB))
    sb = _sigmoid(seg(C_SB, C_F))
    return qn, kn, v, logf, vbn, ga, bmix, sa, sb


def _in_proj_prompt_kernel(n_alias, *refs):
    (x_ref, g_ref, w_ref, bf_ref, qgt_ref, kgt_ref, vg_ref, wcat_ref, sbias_ref, tri_ref, pk_ref,
     pq_ref, qt_ref, ka_ref, vt_ref, ko_ref, vo_ref, lf_ref, ga_ref, bm_ref, sa_ref, sb_ref,
     carry_ref) = refs[n_alias:]
    sr = vt_ref.shape[-1]

    @pl.when(pl.program_id(1) == 0)
    def _():
        carry_ref[...] = jnp.zeros_like(carry_ref)

    def head_norm_t(zt, gain_t):
        sq = zt * zt
        parts = []
        for hh in range(2):
            ms = jnp.sum(sq[hh * HEAD_DIM:(hh + 1) * HEAD_DIM], axis=0, keepdims=True) * (1.0 / HEAD_DIM)
            parts.append(jnp.broadcast_to(lax.rsqrt(ms + EPS), (HEAD_DIM, sr)))
        return zt * jnp.concatenate(parts, axis=0) * gain_t

    carry = carry_ref[...]
    lane = lax.broadcasted_iota(jnp.int32, (sr, LANES), 1)
    row8 = lax.broadcasted_iota(jnp.int32, (N_HEADS, sr), 0)
    pad_rows = jnp.zeros((LANES - N_HEADS, sr), F32)
    ones_blk = jnp.where(lax.broadcasted_iota(jnp.int32, (VT_ROWS - HEAD_DIM, sr), 0) == 0, 1.0, 0.0)
    for sub in range(x_ref.shape[-2] // sr):
        rows = slice(sub * sr, (sub + 1) * sr)
        zq, zk, v, logf, _, ga, bmix, sa, sb = _in_proj_common(
            x_ref[0, rows], g_ref, w_ref, bf_ref, None, None, vg_ref, None, wcat_ref, sbias_ref, CHUNK)

        lft = logf.T[0:N_HEADS]
        lf_ref[0, 0, :, rows] = lft
        ga_ref[0, rows] = ga.astype(BF16)
        bm_ref[0, rows] = bmix.astype(BF16)
        sa_ref[0, rows] = sa.astype(BF16)
        sb_ref[0, rows] = sb.astype(BF16)

        stacked = jnp.concatenate([part.astype(F32) for part in _split3(lft)]
                                  + [jnp.zeros((N_HEADS, sr), F32)], axis=0).astype(BF16)
        cs = _dot(stacked, tri_ref[...])
        ct = cs[0:N_HEADS] + cs[N_HEADS:2 * N_HEADS] + cs[2 * N_HEADS:3 * N_HEADS] + carry
        carry = jnp.broadcast_to(ct[:, sr - 1:sr], (N_HEADS, sr))
        c2t = ct * LOG2E
        c2 = jnp.concatenate([c2t, pad_rows], axis=0).T

        hi, mid, lo = _split3(c2)
        cp = (hi.astype(F32) + pltpu.roll(mid.astype(F32), 8, axis=1)
              + pltpu.roll(lo.astype(F32), 16, axis=1) + jnp.where(lane == 24, 1.0, 0.0))
        extra_k = _dot(cp.astype(BF16), pk_ref[...])

        hit, midt, lot = (part.astype(F32) for part in _split3(c2t))
        cpt = jnp.concatenate(
            [hit, midt, lot, jnp.where(row8 == 0, 1.0, 0.0), jnp.zeros((LANES - 32, sr), F32)],
            axis=0).astype(BF16)
        extra_q = _dot(pq_ref[...], cpt)

        for j in range(N_HEADS // 2):
            qs_t = head_norm_t(zq[:, j * LANES:(j + 1) * LANES].T, qgt_ref[...])
            ks_t = head_norm_t(zk[:, j * LANES:(j + 1) * LANES].T, kgt_ref[...])
            vs_t = v[:, j * LANES:(j + 1) * LANES].T
            ks = ks_t.T
            ks_sw = pltpu.roll(ks, HEAD_DIM, axis=1)
            ko_ref[0, 0, j * LANES:(j + 1) * LANES, rows] = ks_t
            vo_ref[0, 0, j * LANES:(j + 1) * LANES, rows] = vs_t
            for hh in range(2):
                hd = 2 * j + hh
                qt_ref[0, hd, :, rows] = jnp.concatenate(
                    [qs_t[hh * HEAD_DIM:(hh + 1) * HEAD_DIM], extra_q[hd * HEAD_DIM:(hd + 1) * HEAD_DIM]],
                    axis=0).astype(BF16)
                vt_ref[0, hd, sub] = jnp.concatenate(
                    [vs_t[hh * HEAD_DIM:(hh + 1) * HEAD_DIM], ones_blk], axis=0).astype(BF16)
                ka_ref[0, hd, rows] = jnp.where(lane < HEAD_DIM, ks if hh == 0 else ks_sw,
                                                extra_k).astype(BF16)
    carry_ref[...] = carry


def _in_proj_sample_kernel(x_ref, g_ref, w_ref, bf_ref, qg_ref, kg_ref, vg_ref, bd_ref, wcat_ref,
                           sbias_ref,
                           q_ref, ko_ref, vo_ref, lf_ref, lfp_ref, vbn_ref, ga_ref, bm_ref, sa_ref, sb_ref,
                           *, seq_local):
    qn, kn, v, logf, vbn, ga, bmix, sa, sb = _in_proj_common(
        x_ref[...], g_ref, w_ref, bf_ref, qg_ref, kg_ref, vg_ref, bd_ref, wcat_ref, sbias_ref, seq_local)
    q_ref[...] = qn * (HEAD_DIM ** -0.5 * LOG2E)
    ko_ref[...] = kn
    vo_ref[...] = v
    lf_ref[...] = logf[:, :N_HEADS]
    lfp_ref[...] = logf
    vbn_ref[...] = vbn
    ga_ref[...] = ga.astype(BF16)
    bm_ref[...] = bmix.astype(BF16)
    sa_ref[...] = sa.astype(BF16)
    sb_ref[...] = sb.astype(BF16)


def _const_spec(arr):
    if isinstance(arr, tuple):
        return tuple(_const_spec(a) for a in arr)
    return pl.BlockSpec(arr.shape, lambda *_: (0,) * arr.ndim, pipeline_mode=pl.Buffered(1))


def _in_proj_consts(tm):
    bd = np.kron(np.eye(N_HEADS, dtype=np.float32), np.ones((HEAD_DIM, HEAD_DIM), np.float32))
    tri = np.triu(np.ones((tm, tm), np.float32))
    pk = np.zeros((LANES, LANES), np.float32)
    pq = np.zeros((N_HEADS * HEAD_DIM, LANES), np.float32)
    for hd in range(N_HEADS):
        for part in range(3):
            pk[8 * part + hd, HEAD_DIM + 3 * hd + part] = -1.0
            pk[24, HEAD_DIM + 3 * N_HEADS + part] = 1.0
            pq[hd * HEAD_DIM + 3 * hd + part, 24] = 1.0
            pq[hd * HEAD_DIM + 3 * N_HEADS + part, 8 * part + hd] = 1.0
    return (jnp.asarray(bd, BF16), jnp.asarray(tri, BF16), jnp.asarray(pk, BF16), jnp.asarray(pq, BF16))


def _in_proj_prompt(x, params, tm, slab, layer, depth, stacked):
    b, s, _ = x.shape
    nt = s // tm
    _, tri, pk, pq = _in_proj_consts(slab)
    g, w, bf, qgt, kgt, vg, wcat, sbias = params
    consts = (g, w, bf, qgt, kgt, vg, wcat, sbias, tri, pk, pq)
    stacked = () if stacked is None else tuple(stacked)
    row3 = lambda width: pl.BlockSpec((1, tm, width), lambda bi, i: (bi, i, 0))
    col4 = lambda rows: pl.BlockSpec((1, 1, rows, tm), lambda bi, i: (layer, bi, 0, i))
    out_shape = (
        jax.ShapeDtypeStruct((b, N_HEADS, LANES, s), BF16),
        jax.ShapeDtypeStruct((b, N_HEADS, s, LANES), BF16),
        jax.ShapeDtypeStruct((b, N_HEADS, s // slab, VT_ROWS, slab), BF16),
        jax.ShapeDtypeStruct((depth, b, WIDTH_A, s), F32),
        jax.ShapeDtypeStruct((depth, b, WIDTH_A, s), F32),
        jax.ShapeDtypeStruct((depth, b, N_HEADS, s), F32),
        jax.ShapeDtypeStruct((b, s, WIDTH_A), BF16),
        jax.ShapeDtypeStruct((b, s, WIDTH_B), BF16),
        jax.ShapeDtypeStruct((b, s, D_MODEL), BF16),
        jax.ShapeDtypeStruct((b, s, D_MODEL), BF16),
    )
    out_specs = (
        pl.BlockSpec((1, N_HEADS, LANES, tm), lambda bi, i: (bi, 0, 0, i)),
        pl.BlockSpec((1, N_HEADS, tm, LANES), lambda bi, i: (bi, 0, i, 0)),
        pl.BlockSpec((1, N_HEADS, tm // slab, VT_ROWS, slab), lambda bi, i: (bi, 0, i, 0, 0)),
        col4(WIDTH_A), col4(WIDTH_A), col4(N_HEADS), row3(WIDTH_A), row3(WIDTH_B),
        row3(D_MODEL), row3(D_MODEL),
    )
    return pl.pallas_call(
        functools.partial(_in_proj_prompt_kernel, len(stacked)),
        out_shape=out_shape,
        grid=(b, nt),
        in_specs=[pl.BlockSpec(memory_space=pl.ANY)] * len(stacked) + [row3(D_MODEL)]
        + [_const_spec(c) for c in consts],
        out_specs=out_specs,
        scratch_shapes=[pltpu.VMEM((N_HEADS, slab), F32)],
        input_output_aliases={n: 3 + n for n in range(len(stacked))},
        compiler_params=pltpu.CompilerParams(
            dimension_semantics=("arbitrary", "arbitrary"), vmem_limit_bytes=VMEM_LIMIT),
        name="in_proj_prompt",
    )(*stacked, x, *consts)


def _in_proj_sample(x, params, tm, seq_local):
    n = x.shape[0]
    bd = _in_proj_consts(tm)[0]
    g, w, bf, qg, kg, vg, wcat, sbias = params
    consts = (g, w, bf, qg, kg, vg, bd, wcat, sbias)
    row2 = lambda width: pl.BlockSpec((tm, width), lambda i: (i, 0))
    out_shape = (
        jax.ShapeDtypeStruct((n, WIDTH_A), F32),
        jax.ShapeDtypeStruct((n, WIDTH_A), F32),
        jax.ShapeDtypeStruct((n, WIDTH_A), F32),
        jax.ShapeDtypeStruct((n, N_HEADS), F32),
        jax.ShapeDtypeStruct((n, LANES), F32),
        jax.ShapeDtypeStruct((n, WIDTH_B), F32),
        jax.ShapeDtypeStruct((n, WIDTH_A), BF16),
        jax.ShapeDtypeStruct((n, WIDTH_B), BF16),
        jax.ShapeDtypeStruct((n, D_MODEL), BF16),
        jax.ShapeDtypeStruct((n, D_MODEL), BF16),
    )
    out_specs = (row2(WIDTH_A), row2(WIDTH_A), row2(WIDTH_A), row2(N_HEADS), row2(LANES), row2(WIDTH_B),
                 row2(WIDTH_A), row2(WIDTH_B), row2(D_MODEL), row2(D_MODEL))
    return pl.pallas_call(
        functools.partial(_in_proj_sample_kernel, seq_local=seq_local),
        out_shape=out_shape,
        grid=(n // tm,),
        in_specs=[row2(D_MODEL)] + [_const_spec(c) for c in consts],
        out_specs=out_specs,
        compiler_params=pltpu.CompilerParams(
            dimension_semantics=("arbitrary",), vmem_limit_bytes=VMEM_LIMIT),
        name="in_proj_sample",
    )(x, *consts)


def _attn_prompt_kernel(qt_ref, ka_ref, vt_ref, o_ref, s_ref, m_ref, al_ref, acc_ref):
    g = qt_ref.shape[1]
    tq = qt_ref.shape[-1]
    t = vt_ref.shape[-1]
    assert tq == 2 * t
    i = pl.program_id(2)
    key = lax.broadcasted_iota(jnp.int32, (t, tq), 0)
    qry = lax.broadcasted_iota(jnp.int32, (t, tq), 1)

    def scores(hd, blk, slot, mask, m_old):
        k = ka_ref[0, hd, pl.ds(pl.multiple_of(blk * t, t), t), :]
        s = _dot(k, qt_ref[0, hd])
        if mask is not None:
            s = jnp.where(mask, s, NEG_BIG)
        s_ref[slot, hd] = s
        m_new = jnp.maximum(m_old, jnp.max(s, axis=0, keepdims=True))
        m_ref[slot, hd, 0:1, :] = m_new
        al_ref[slot, hd, 0:1, :] = jnp.exp2(m_old - m_new)

    def iteration(j, cur, next_mask, has_next):
        for hd in range(g):
            p = jnp.exp2(s_ref[cur, hd] - m_ref[cur, hd, 0:1, :]).astype(BF16)
            if has_next:
                scores(hd, j + 1, 1 - cur, next_mask, m_ref[cur, hd, 0:1, :])
            acc_ref[hd] = al_ref[cur, hd, 0:1, :] * acc_ref[hd] + _dot(vt_ref[0, hd, j], p)

    acc_ref[...] = jnp.zeros_like(acc_ref)
    diag_lo = key <= qry
    diag_hi = key + t <= qry
    for hd in range(g):
        scores(hd, 0, 0, key <= qry + jnp.where(i > 0, tq, 0), jnp.full((1, tq), NEG_BIG, F32))

    def pair(jj, carry):
        iteration(2 * jj, 0, None, True)
        iteration(2 * jj + 1, 1, None, True)
        return carry

    lax.fori_loop(0, i - 1, pair, 0)

    @pl.when(i >= 1)
    def _():
        iteration(2 * i - 2, 0, None, True)
        iteration(2 * i - 1, 1, diag_lo, True)

    last = 2 * i + 1
    iteration(last - 1, 0, diag_hi, True)
    iteration(last, 1, None, False)
    for hd in range(g):
        acc = acc_ref[hd]
        o_ref[0, hd * HEAD_DIM:(hd + 1) * HEAD_DIM, :] = acc[0:HEAD_DIM] / acc[HEAD_DIM:HEAD_DIM + 1]


def _attn_prompt(qt, ka, vt, g):
    b, h, _, s = qt.shape
    t = vt.shape[-1]
    tq = 2 * t
    return pl.pallas_call(
        _attn_prompt_kernel,
        out_shape=jax.ShapeDtypeStruct((b, WIDTH_A, s), F32),
        grid=(b, h // g, s // tq),
        in_specs=[
            pl.BlockSpec((1, g, LANES, tq), lambda bi, hi, i: (bi, hi, 0, i)),
            pl.BlockSpec((1, g, s, LANES), lambda bi, hi, i: (bi, hi, 0, 0),
                         pipeline_mode=pl.Buffered(1)),
            pl.BlockSpec((1, g, s // t, VT_ROWS, t), lambda bi, hi, i: (bi, hi, 0, 0, 0),
                         pipeline_mode=pl.Buffered(1)),
        ],
        out_specs=pl.BlockSpec((1, g * HEAD_DIM, tq), lambda bi, hi, i: (bi, hi, i)),
        scratch_shapes=[pltpu.VMEM((2, g, t, tq), F32),
                        pltpu.VMEM((2, g, 8, tq), F32), pltpu.VMEM((2, g, 8, tq), F32),
                        pltpu.VMEM((g, VT_ROWS, tq), F32)],
        compiler_params=pltpu.CompilerParams(
            dimension_semantics=("arbitrary", "arbitrary", "arbitrary"), vmem_limit_bytes=VMEM_LIMIT),
        name="attn_prompt",
    )(qt, ka, vt)


def _attn_sample_kernel(n_pages, t_new, seqs, pt_ref, q_ref, kn_ref, vn_ref, ln_ref, ls_ref, ones_ref, m3_ref,
                        *rest):
    del pt_ref
    n_all = seqs * n_pages
    o_ref = rest[3 * n_all]
    for u in range(seqs):
        tok = slice(u * t_new, (u + 1) * t_new)
        pages = [rest[grp * n_all + u * n_pages:grp * n_all + (u + 1) * n_pages] for grp in range(3)]
        o_ref[tok] = _attn_sample_one(q_ref[tok], kn_ref[tok], vn_ref[tok], ln_ref[tok],
                                      ls_ref, ones_ref, m3_ref, *pages)


def _attn_sample_one(q, k_new, v_new, logf_new, ls_ref, ones_ref, m3_ref, k_refs, v_refs, l_refs):
    t_new = q.shape[0]
    n_pages = len(k_refs)
    rows = t_new * N_HEADS

    def per_token(x):
        return jnp.concatenate(
            [jnp.broadcast_to(x[tk:tk + 1], (N_HEADS, x.shape[1])) for tk in range(t_new)], axis=0)

    row = lax.broadcasted_iota(jnp.int32, (rows, WIDTH_A), 0)
    lane = lax.broadcasted_iota(jnp.int32, (rows, WIDTH_A), 1)
    own_head = (row & (N_HEADS - 1)) == (lane >> HEAD_DIM_BITS)
    qbd = jnp.where(own_head, per_token(q), 0.0).astype(BF16)

    cn = logf_new * LOG2E
    trow = lax.broadcasted_iota(jnp.int32, (t_new, LANES), 0)
    sh = 1
    while sh < t_new:
        cn = cn + jnp.where(trow >= sh, pltpu.roll(cn, sh, axis=0), 0.0)
        sh *= 2
    row1 = lax.broadcasted_iota(jnp.int32, (rows, LANES), 0)
    lane1 = lax.broadcasted_iota(jnp.int32, (rows, LANES), 1)
    head_lane = (row1 & (N_HEADS - 1)) == lane1
    cn_col = jnp.sum(jnp.where(head_lane, per_token(cn), 0.0), axis=1, keepdims=True)

    x = jnp.concatenate([l_refs[p][0, 0] for p in range(n_pages)], axis=0) * LOG2E
    xp = _split3(x)
    r_in = _dot3_right(xp, ls_ref[...])
    tot_b = _dot3_right(xp, ones_ref[...])
    r_page = _dot3_left(m3_ref[...], _split3(tot_b))
    r = r_in + r_page

    s_blocks = []
    for p in range(n_pages):
        kp_t = k_refs[p][0, 0].astype(BF16)
        r_p = r[p * N_HEADS:(p + 1) * N_HEADS]
        s_blocks.append(_dot(qbd, kp_t) + jnp.concatenate([r_p] * t_new, axis=0) + cn_col)
    pad = jnp.zeros((PAGE - t_new, WIDTH_A), F32)
    kn = jnp.concatenate([k_new, pad], axis=0).astype(BF16)
    vn = jnp.concatenate([v_new, pad], axis=0).astype(BF16)
    cn_pad = jnp.concatenate([cn, jnp.zeros((PAGE - t_new, LANES), F32)], axis=0)
    sel = jnp.where(head_lane, 1.0, 0.0).astype(BF16)
    cn_keys = sum(_dot_nt(sel, part) for part in _split3(cn_pad))
    s_new = _dot_nt(qbd, kn) + cn_col - cn_keys
    s_new = jnp.where((lane1 <= (row1 >> HEAD_BITS)) & (lane1 < t_new), s_new, NEG_BIG)
    s_blocks.append(s_new)

    m = s_blocks[0]
    for sb in s_blocks[1:]:
        m = jnp.maximum(m, sb)
    m = jnp.max(m, axis=1, keepdims=True)
    l_sum = jnp.zeros((rows, 1), F32)
    o = jnp.zeros((rows, WIDTH_A), F32)
    for p in range(n_pages + 1):
        pe = jnp.exp2(s_blocks[p] - m)
        l_sum = l_sum + jnp.sum(pe, axis=1, keepdims=True)
        if p < n_pages:
            o = o + _dot_nt(pe.astype(BF16), v_refs[p][0, 0].astype(BF16))
        else:
            o = o + _dot(pe.astype(BF16), vn)
    o = jnp.where(own_head, o / l_sum, 0.0)
    return jnp.concatenate(
        [jnp.sum(o[tk * N_HEADS:(tk + 1) * N_HEADS], axis=0, keepdims=True) for tk in range(t_new)], axis=0)


def _attn_sample_consts(n_pages):
    key = np.arange(PAGE)
    ls = (key[:, None] > key[None, :]).astype(np.float32)
    ones = np.ones((PAGE, LANES), np.float32)
    ph = np.arange(n_pages * N_HEADS)
    m3 = ((ph[:, None] % N_HEADS == ph[None, :] % N_HEADS)
          & (ph[None, :] // N_HEADS > ph[:, None] // N_HEADS)).astype(np.float32)
    return tuple(jnp.asarray(a, BF16) for a in (ls, ones, m3))


def _merge_kernel(transposed, slab, a_ref, ga_ref, bm_ref, sa_ref, sb_ref, x_ref, wpa_ref, wpb_ref, wo_ref,
                  y_ref):
    for sub in range(x_ref.shape[-2] // slab):
        rows = slice(sub * slab, (sub + 1) * slab)
        at = (lambda ref: ref[0, rows]) if transposed else (lambda ref: ref[rows])
        attn = a_ref[0, :, rows].T if transposed else a_ref[rows]
        a = (attn * at(ga_ref).astype(F32)).astype(BF16)
        pa = _dot(a, wpa_ref[...])
        pb = _dot(at(bm_ref), wpb_ref[...])
        m = at(sa_ref).astype(F32) * pa + at(sb_ref).astype(F32) * pb
        y = at(x_ref) + _dot(m.astype(BF16), wo_ref[...])
        if transposed:
            y_ref[0, rows] = y
        else:
            y_ref[rows] = y


def _merge_attn_kernel(n_pages, t_new, seqs, pt_ref, a_ref, ga_ref, bm_ref, sa_ref, sb_ref, x_ref,
                       wpa_ref, wpb_ref, wo_ref, q_ref, kn_ref, vn_ref, ln_ref, ls_ref, ones_ref, m3_ref,
                       *rest):
    n_all = 3 * seqs * n_pages
    y_ref, o_ref = rest[n_all], rest[n_all + 1]
    _merge_kernel(True, x_ref.shape[-2], a_ref, ga_ref, bm_ref, sa_ref, sb_ref, x_ref, wpa_ref, wpb_ref,
                  wo_ref, y_ref)
    _attn_sample_kernel(n_pages, t_new, seqs, pt_ref, q_ref, kn_ref, vn_ref, ln_ref, ls_ref, ones_ref,
                        m3_ref, *rest[:n_all], o_ref)


def _merge_prompt_attn_sample(layer, page_table, attn_t, ga, bm, sa, sb, x, wpa, wpb, wo, q, kn, vn, lnp,
                              ck, cv, clt, t_new, seqs):
    b, s, _ = x.shape
    n_seq, n_pages = page_table.shape
    nt = n_seq // (seqs * b)
    tm = s // nt
    assert b * nt * seqs == n_seq and nt * tm == s
    consts = _attn_sample_consts(n_pages)
    step = lambda bi, i: bi * nt + i
    row3 = lambda width: pl.BlockSpec((1, tm, width), lambda bi, i, pt: (bi, i, 0))
    tok = lambda width: pl.BlockSpec((seqs * t_new, width), lambda bi, i, pt: (step(bi, i), 0))

    def page_specs(block):
        return [pl.BlockSpec(block, functools.partial(
            lambda bi, i, pt, u, p: (layer, pt[step(bi, i) * seqs + u, p], 0, 0), u=u, p=p))
                for u in range(seqs) for p in range(n_pages)]

    grid_spec = pltpu.PrefetchScalarGridSpec(
        num_scalar_prefetch=1,
        grid=(b, nt),
        in_specs=[pl.BlockSpec((1, WIDTH_A, tm), lambda bi, i, pt: (bi, 0, i)),
                  row3(WIDTH_A), row3(WIDTH_B), row3(D_MODEL), row3(D_MODEL), row3(D_MODEL),
                  _const_spec(wpa), _const_spec(wpb), _const_spec(wo),
                  tok(WIDTH_A), tok(WIDTH_A), tok(WIDTH_A), tok(LANES)]
        + [_const_spec(c) for c in consts]
        + page_specs((1, 1, WIDTH_A, PAGE)) + page_specs((1, 1, WIDTH_A, PAGE))
        + page_specs((1, 1, N_HEADS, PAGE)),
        out_specs=(row3(D_MODEL), tok(WIDTH_A)),
    )
    pages = seqs * n_pages
    return pl.pallas_call(
        functools.partial(_merge_attn_kernel, n_pages, t_new, seqs),
        out_shape=(jax.ShapeDtypeStruct((b, s, D_MODEL), F32),
                   jax.ShapeDtypeStruct((n_seq * t_new, WIDTH_A), F32)),
        grid_spec=grid_spec,
        compiler_params=pltpu.CompilerParams(
            dimension_semantics=("arbitrary", "arbitrary"), vmem_limit_bytes=VMEM_LIMIT),
        name="merge_prompt_attn_sample",
    )(page_table, attn_t, ga, bm, sa, sb, x, wpa, wpb, wo, q, kn, vn, lnp, *consts,
      *([ck] * pages), *([cv] * pages), *([clt] * pages))


def _merge_sample(attn, ga, bm, sa, sb, x, wpa, wpb, wo, tm):
    n = x.shape[0]
    row2 = lambda width: pl.BlockSpec((tm, width), lambda i: (i, 0))
    return pl.pallas_call(
        functools.partial(_merge_kernel, False, tm),
        out_shape=jax.ShapeDtypeStruct((n, D_MODEL), F32),
        grid=(n // tm,),
        in_specs=[row2(WIDTH_A), row2(WIDTH_A), row2(WIDTH_B), row2(D_MODEL), row2(D_MODEL), row2(D_MODEL),
                  _const_spec(wpa), _const_spec(wpb), _const_spec(wo)],
        out_specs=row2(D_MODEL),
        compiler_params=pltpu.CompilerParams(
            dimension_semantics=("arbitrary",), vmem_limit_bytes=VMEM_LIMIT),
        name="merge_sample",
    )(attn, ga, bm, sa, sb, x, wpa, wpb, wo)


KEY_BLOCK = 256
TM_PROMPT = 512
HEADS_PER_STEP = 8
SEQS_PER_STEP = 2
TM_SAMPLE = 256


def kernel(x_prompt, x_sample, cache_k, cache_v, cache_logf, page_table, norm_g, w_in, b_forget, q_norm_g,
           k_norm_g, v_norm_g, w_s, b_s, w_proj_a, w_proj_b, w_out):
    depth = w_in.shape[0]
    bsz, seq, _ = x_prompt.shape
    n_seq, t_new, _ = x_sample.shape
    n_pool = cache_k.shape[1]

    f_lo = 4 * WIDTH_A
    w_head = w_in[:, :, :f_lo].astype(BF16)
    w_tail = w_in[:, :, f_lo + N_HEADS:].astype(BF16)
    w_forget = jnp.pad(w_in[:, :, f_lo:f_lo + N_HEADS],
                       ((0, 0), (0, 0), (0, C_END - C_F - N_HEADS))).astype(BF16)
    bf_pad = jnp.pad(b_forget, ((0, 0), (0, LANES - N_HEADS)))[:, None, :]
    qg = jnp.tile(q_norm_g, (1, N_HEADS))[:, None, :]
    kg = jnp.tile(k_norm_g, (1, N_HEADS))[:, None, :]
    gain_cols = lambda gn: jnp.tile(gn[:, :, None], (1, 2, KEY_BLOCK))
    qg_t = gain_cols(q_norm_g * (HEAD_DIM ** -0.5 * LOG2E))
    kg_t = gain_cols(k_norm_g)
    vg = v_norm_g[:, None, :]
    ng = norm_g[:, None, :]
    wcat_p = w_s.transpose(0, 2, 1, 3).reshape(depth, CHUNK, N_GROUPS * CHUNK).astype(BF16)
    w_small = jnp.tile(w_s[:, :, :t_new, :t_new], (1, 1, CHUNK // t_new, CHUNK // t_new))
    wcat_s = w_small.transpose(0, 2, 1, 3).reshape(depth, CHUNK, N_GROUPS * CHUNK).astype(BF16)
    sbias_p = jnp.repeat(b_s.transpose(0, 2, 1), GROUP_DIM, axis=-1)
    sbias_s = jnp.repeat(jnp.tile(b_s[:, :, :t_new], (1, 1, CHUNK // t_new)).transpose(0, 2, 1),
                         GROUP_DIM, axis=-1)
    wpa = w_proj_a.astype(BF16)
    wpb = w_proj_b.astype(BF16)
    wo = w_out.astype(BF16)
    ck = cache_k.transpose(0, 1, 3, 4, 2).reshape(depth, n_pool, WIDTH_A, PAGE)
    cv = cache_v.transpose(0, 1, 3, 4, 2).reshape(depth, n_pool, WIDTH_A, PAGE)
    clt = cache_logf.transpose(0, 1, 3, 2)

    xp = x_prompt
    xs = x_sample.reshape(n_seq * t_new, D_MODEL)
    outs = [[] for _ in range(4)]
    stacked = None
    for l in range(depth):
        weights = (w_head[l], w_tail[l], w_forget[l])
        qt, ka, vt, *stacked, ga, bm, sa, sb = _in_proj_prompt(
            xp, (ng[l], weights, bf_pad[l], qg_t[l], kg_t[l], vg[l], wcat_p[l], sbias_p[l]),
            TM_PROMPT, KEY_BLOCK, l, depth, stacked)
        common = (ng[l], weights, bf_pad[l], qg[l], kg[l], vg[l])
        attn_t = _attn_prompt(qt, ka, vt, HEADS_PER_STEP)

        q_s, k_s, v_s, lf_s, lfp_s, vbn_s, ga_s, bm_s, sa_s, sb_s = _in_proj_sample(
            xs, common + (wcat_s[l], sbias_s[l]), TM_SAMPLE, t_new)
        xp, attn_s = _merge_prompt_attn_sample(
            l, page_table, attn_t, ga, bm, sa, sb, xp, wpa[l], wpb[l], wo[l], q_s, k_s, v_s, lfp_s,
            ck, cv, clt, t_new, SEQS_PER_STEP)
        xs = _merge_sample(attn_s, ga_s, bm_s, sa_s, sb_s, xs, wpa[l], wpb[l], wo[l], TM_SAMPLE)
        outs[0].append(k_s.reshape(n_seq, t_new, N_HEADS, HEAD_DIM))
        outs[1].append(v_s.reshape(n_seq, t_new, N_HEADS, HEAD_DIM))
        outs[2].append(lf_s.reshape(n_seq, t_new, N_HEADS))
        outs[3].append(vbn_s.reshape(n_seq, t_new, WIDTH_B))
    kt_all, vt_all, lft_all = stacked
    unstack = lambda a: a.reshape(depth, bsz, N_HEADS, HEAD_DIM, seq).transpose(0, 1, 4, 2, 3)
    return ((xp, xs.reshape(n_seq, t_new, D_MODEL), unstack(kt_all), unstack(vt_all),
             lft_all.transpose(0, 1, 3, 2)) + tuple(jnp.stack(o) for o in outs))
```

```python
import functools
import math

import numpy as np
import jax
import jax.numpy as jnp
from jax import lax
from jax.experimental import pallas as pl
from jax.experimental.pallas import tpu as pltpu

D_MODEL = 1024
N_HEADS = 8
HEAD_DIM = 64
HEAD_BITS = 3
HEAD_DIM_BITS = 6
WIDTH_A = N_HEADS * HEAD_DIM
N_GROUPS = 8
GROUP_DIM = 64
GROUP_DIM_BITS = 6
WIDTH_B = N_GROUPS * GROUP_DIM
CHUNK = 128
PAGE = 128
EPS = 1e-6
LOG2E = math.log2(math.e)
NEG_BIG = -1e30

LANES = 128
VT_ROWS = 80
VMEM_LIMIT = 56 * 1024 * 1024

C_Q, C_K, C_V, C_GA, C_U, C_VB, C_GB, C_SA, C_SB, C_F, C_END = (
    0, 512, 1024, 1536, 2048, 2560, 3072, 3584, 4608, 5632, 5760)

F32 = jnp.float32
BF16 = jnp.bfloat16


def _split3(x):
    hi = x.astype(BF16)
    r1 = x - hi.astype(F32)
    mid = r1.astype(BF16)
    lo = (r1 - mid.astype(F32)).astype(BF16)
    return hi, mid, lo


def _dot(a, b):
    return jnp.dot(a, b, preferred_element_type=F32)


def _dot_nt(a, b):
    return lax.dot_general(a, b, (((1,), (1,)), ((), ())), preferred_element_type=F32)


def _dot3_right(parts, w):
    return _dot(parts[0], w) + _dot(parts[1], w) + _dot(parts[2], w)


def _dot3_left(w, parts):
    return _dot(w, parts[0]) + _dot(w, parts[1]) + _dot(w, parts[2])


def _sigmoid(x):
    return 1.0 / (1.0 + jnp.exp(-x))


def _gelu_tanh(x):
    c = math.sqrt(2.0 / math.pi)
    return 0.5 * x * (1.0 + jnp.tanh(c * (x + 0.044715 * (x * x * x))))


def _log_sigmoid(x):
    return jnp.minimum(x, 0.0) - jnp.log1p(jnp.exp(-jnp.abs(x)))


def _in_proj_common(x, g_ref, w_ref, bf_ref, qg_ref, kg_ref, vg_ref, bd_ref, wcat_ref, sbias_ref,
                    seq_local):
    tm = x.shape[0]
    ms = jnp.mean(x * x, axis=-1, keepdims=True)
    h = ((x * lax.rsqrt(ms + EPS)) * g_ref[...]).astype(BF16)

    def seg(lo, hi):
        w_head, w_tail, w_forget = w_ref
        if hi <= C_U:
            return _dot(h, w_head[:, lo:hi])
        if lo >= C_F:
            return _dot(h, w_forget[...])
        return _dot(h, w_tail[:, lo - C_U:hi - C_U])

    def head_norm(z, gain):
        msz = _dot((z * z).astype(BF16), bd_ref[...]) * (1.0 / HEAD_DIM)
        return z * lax.rsqrt(msz + EPS) * gain

    if bd_ref is None:
        qn, kn = seg(C_Q, C_K), seg(C_K, C_V)
    else:
        qn = head_norm(seg(C_Q, C_K), qg_ref[...])
        kn = head_norm(seg(C_K, C_V), kg_ref[...])
    v = seg(C_V, C_GA)
    zga = seg(C_GA, C_U)
    ga = zga * _sigmoid(zga)

    lane = lax.broadcasted_iota(jnp.int32, (tm, LANES), 1)
    logf = jnp.where(lane < N_HEADS, _log_sigmoid(seg(C_F, C_END) + bf_ref[...]), 0.0)

    u = _gelu_tanh(seg(C_U, C_VB))
    vbg = _gelu_tanh(seg(C_VB, C_GB))
    msv = jnp.mean(vbg * vbg, axis=-1, keepdims=True)
    vbn = vbg * lax.rsqrt(msv + EPS) * vg_ref[...]
    zgb = seg(C_GB, C_SA)
    ugb = u * (zgb * _sigmoid(zgb))

    t_idx = lax.broadcasted_iota(jnp.int32, (CHUNK, N_GROUPS * CHUNK), 0)
    s_idx = lax.broadcasted_iota(jnp.int32, (CHUNK, N_GROUPS * CHUNK), 1) & (CHUNK - 1)
    keep = s_idx <= t_idx
    if seq_local < CHUNK:
        shift = int(math.log2(seq_local))
        keep = keep & ((s_idx >> shift) == (t_idx >> shift))
    wm = jnp.where(keep, wcat_ref[...], jnp.zeros((), BF16))
    gq = N_GROUPS // 2
    glane = lax.broadcasted_iota(jnp.int32, (CHUNK, gq * GROUP_DIM), 1) >> GROUP_DIM_BITS
    vbn_b = vbn.astype(BF16)
    s_chunks = []
    for c in range(tm // CHUNK):
        halves = []
        for hf in range(2):
            vc = vbn_b[c * CHUNK:(c + 1) * CHUNK, hf * gq * GROUP_DIM:(hf + 1) * gq * GROUP_DIM]
            rhs = jnp.concatenate(
                [jnp.where(glane == g, vc, jnp.zeros((), BF16)) for g in range(gq)], axis=0)
            halves.append(_dot(wm[:, hf * gq * CHUNK:(hf + 1) * gq * CHUNK], rhs))
        s_chunks.append(jnp.concatenate(halves, axis=1) + sbias_ref[...])
    s_sgu = jnp.concatenate(s_chunks, axis=0)
    bmix = ugb * s_sgu

    sa = _sigmoid(seg(C_SA, C_SB))
    sb = _sigmoid(seg(C_SB, C_F))
    return qn, kn, v, logf, vbn, ga, bmix, sa, sb


def _in_proj_prompt_kernel(layer, *refs):
    n_prev = 3 if layer else 0
    (x_ref, g_ref, w_ref, bf_ref, qgt_ref, kgt_ref, vg_ref, wcat_ref, sbias_ref, tri_ref, pk_ref,
     pq_ref, qt_ref, ka_ref, vt_ref, ko_ref, vo_ref, lf_ref, ga_ref, bm_ref, sa_ref, sb_ref,
     carry_ref) = refs[n_prev:]
    for prev_ref, out_ref in zip(refs[:n_prev], (ko_ref, vo_ref, lf_ref)):
        out_ref[0:layer] = prev_ref[...]
    sr = vt_ref.shape[-1]

    @pl.when(pl.program_id(1) == 0)
    def _():
        carry_ref[...] = jnp.zeros_like(carry_ref)

    def head_norm_t(zt, gain_t):
        sq = zt * zt
        parts = []
        for hh in range(2):
            ms = jnp.sum(sq[hh * HEAD_DIM:(hh + 1) * HEAD_DIM], axis=0, keepdims=True) * (1.0 / HEAD_DIM)
            parts.append(jnp.broadcast_to(lax.rsqrt(ms + EPS), (HEAD_DIM, sr)))
        return zt * jnp.concatenate(parts, axis=0) * gain_t

    carry = carry_ref[...]
    lane = lax.broadcasted_iota(jnp.int32, (sr, LANES), 1)
    row8 = lax.broadcasted_iota(jnp.int32, (N_HEADS, sr), 0)
    pad_rows = jnp.zeros((LANES - N_HEADS, sr), F32)
    ones_blk = jnp.where(lax.broadcasted_iota(jnp.int32, (VT_ROWS - HEAD_DIM, sr), 0) == 0, 1.0, 0.0)
    for sub in range(x_ref.shape[-2] // sr):
        rows = slice(sub * sr, (sub + 1) * sr)
        zq, zk, v, logf, _, ga, bmix, sa, sb = _in_proj_common(
            x_ref[0, rows], g_ref, w_ref, bf_ref, None, None, vg_ref, None, wcat_ref, sbias_ref, CHUNK)

        lft = logf.T[0:N_HEADS]
        lf_ref[layer, 0, :, rows] = lft
        ga_ref[0, rows] = ga.astype(BF16)
        bm_ref[0, rows] = bmix.astype(BF16)
        sa_ref[0, rows] = sa.astype(BF16)
        sb_ref[0, rows] = sb.astype(BF16)

        stacked = jnp.concatenate([part.astype(F32) for part in _split3(lft)]
                                  + [jnp.zeros((N_HEADS, sr), F32)], axis=0).astype(BF16)
        cs = _dot(stacked, tri_ref[...])
        ct = cs[0:N_HEADS] + cs[N_HEADS:2 * N_HEADS] + cs[2 * N_HEADS:3 * N_HEADS] + carry
        carry = jnp.broadcast_to(ct[:, sr - 1:sr], (N_HEADS, sr))
        c2t = ct * LOG2E
        c2 = jnp.concatenate([c2t, pad_rows], axis=0).T

        hi, mid, lo = _split3(c2)
        cp = (hi.astype(F32) + pltpu.roll(mid.astype(F32), 8, axis=1)
              + pltpu.roll(lo.astype(F32), 16, axis=1) + jnp.where(lane == 24, 1.0, 0.0))
        extra_k = _dot(cp.astype(BF16), pk_ref[...])

        hit, midt, lot = (part.astype(F32) for part in _split3(c2t))
        cpt = jnp.concatenate(
            [hit, midt, lot, jnp.where(row8 == 0, 1.0, 0.0), jnp.zeros((LANES - 32, sr), F32)],
            axis=0).astype(BF16)
        extra_q = _dot(pq_ref[...], cpt)

        for j in range(N_HEADS // 2):
            qs_t = head_norm_t(zq[:, j * LANES:(j + 1) * LANES].T, qgt_ref[...])
            ks_t = head_norm_t(zk[:, j * LANES:(j + 1) * LANES].T, kgt_ref[...])
            vs_t = v[:, j * LANES:(j + 1) * LANES].T
            ks = ks_t.T
            ks_sw = pltpu.roll(ks, HEAD_DIM, axis=1)
            ko_ref[layer, 0, j * LANES:(j + 1) * LANES, rows] = ks_t
            vo_ref[layer, 0, j * LANES:(j + 1) * LANES, rows] = vs_t
            for hh in range(2):
                hd = 2 * j + hh
                qt_ref[0, hd, :, rows] = jnp.concatenate(
                    [qs_t[hh * HEAD_DIM:(hh + 1) * HEAD_DIM], extra_q[hd * HEAD_DIM:(hd + 1) * HEAD_DIM]],
                    axis=0).astype(BF16)
                vt_ref[0, hd, sub] = jnp.concatenate(
                    [vs_t[hh * HEAD_DIM:(hh + 1) * HEAD_DIM], ones_blk], axis=0).astype(BF16)
                ka_ref[0, hd, rows] = jnp.where(lane < HEAD_DIM, ks if hh == 0 else ks_sw,
                                                extra_k).astype(BF16)
    carry_ref[...] = carry


def _in_proj_sample_kernel(x_ref, g_ref, w_ref, bf_ref, qg_ref, kg_ref, vg_ref, bd_ref, wcat_ref,
                           sbias_ref,
                           q_ref, ko_ref, vo_ref, lf_ref, lfp_ref, vbn_ref, ga_ref, bm_ref, sa_ref, sb_ref,
                           *, seq_local):
    qn, kn, v, logf, vbn, ga, bmix, sa, sb = _in_proj_common(
        x_ref[...], g_ref, w_ref, bf_ref, qg_ref, kg_ref, vg_ref, bd_ref, wcat_ref, sbias_ref, seq_local)
    q_ref[...] = qn * (HEAD_DIM ** -0.5 * LOG2E)
    ko_ref[...] = kn
    vo_ref[...] = v
    lf_ref[...] = logf[:, :N_HEADS]
    lfp_ref[...] = logf
    vbn_ref[...] = vbn
    ga_ref[...] = ga.astype(BF16)
    bm_ref[...] = bmix.astype(BF16)
    sa_ref[...] = sa.astype(BF16)
    sb_ref[...] = sb.astype(BF16)


def _const_spec(arr):
    if isinstance(arr, tuple):
        return tuple(_const_spec(a) for a in arr)
    return pl.BlockSpec(arr.shape, lambda *_: (0,) * arr.ndim, pipeline_mode=pl.Buffered(1))


def _in_proj_consts(tm):
    bd = np.kron(np.eye(N_HEADS, dtype=np.float32), np.ones((HEAD_DIM, HEAD_DIM), np.float32))
    tri = np.triu(np.ones((tm, tm), np.float32))
    pk = np.zeros((LANES, LANES), np.float32)
    pq = np.zeros((N_HEADS * HEAD_DIM, LANES), np.float32)
    for hd in range(N_HEADS):
        for part in range(3):
            pk[8 * part + hd, HEAD_DIM + 3 * hd + part] = -1.0
            pk[24, HEAD_DIM + 3 * N_HEADS + part] = 1.0
            pq[hd * HEAD_DIM + 3 * hd + part, 24] = 1.0
            pq[hd * HEAD_DIM + 3 * N_HEADS + part, 8 * part + hd] = 1.0
    return (jnp.asarray(bd, BF16), jnp.asarray(tri, BF16), jnp.asarray(pk, BF16), jnp.asarray(pq, BF16))


def _in_proj_prompt(x, params, tm, slab, layer, stacked):
    b, s, _ = x.shape
    nt = s // tm
    _, tri, pk, pq = _in_proj_consts(slab)
    g, w, bf, qgt, kgt, vg, wcat, sbias = params
    consts = (g, w, bf, qgt, kgt, vg, wcat, sbias, tri, pk, pq)
    stacked = () if stacked is None else tuple(stacked)
    assert len(stacked) == (3 if layer else 0)
    row3 = lambda width: pl.BlockSpec((1, tm, width), lambda bi, i: (bi, i, 0))
    col4 = lambda n, rows: pl.BlockSpec((n, 1, rows, tm), lambda bi, i: (0, bi, 0, i))
    depth = layer + 1
    out_shape = (
        jax.ShapeDtypeStruct((b, N_HEADS, LANES, s), BF16),
        jax.ShapeDtypeStruct((b, N_HEADS, s, LANES), BF16),
        jax.ShapeDtypeStruct((b, N_HEADS, s // slab, VT_ROWS, slab), BF16),
        jax.ShapeDtypeStruct((depth, b, WIDTH_A, s), F32),
        jax.ShapeDtypeStruct((depth, b, WIDTH_A, s), F32),
        jax.ShapeDtypeStruct((depth, b, N_HEADS, s), F32),
        jax.ShapeDtypeStruct((b, s, WIDTH_A), BF16),
        jax.ShapeDtypeStruct((b, s, WIDTH_B), BF16),
        jax.ShapeDtypeStruct((b, s, D_MODEL), BF16),
        jax.ShapeDtypeStruct((b, s, D_MODEL), BF16),
    )
    out_specs = (
        pl.BlockSpec((1, N_HEADS, LANES, tm), lambda bi, i: (bi, 0, 0, i)),
        pl.BlockSpec((1, N_HEADS, tm, LANES), lambda bi, i: (bi, 0, i, 0)),
        pl.BlockSpec((1, N_HEADS, tm // slab, VT_ROWS, slab), lambda bi, i: (bi, 0, i, 0, 0)),
        col4(depth, WIDTH_A), col4(depth, WIDTH_A), col4(depth, N_HEADS), row3(WIDTH_A), row3(WIDTH_B),
        row3(D_MODEL), row3(D_MODEL),
    )
    prev_specs = [col4(layer, WIDTH_A), col4(layer, WIDTH_A), col4(layer, N_HEADS)] if layer else []
    return pl.pallas_call(
        functools.partial(_in_proj_prompt_kernel, layer),
        out_shape=out_shape,
        grid=(b, nt),
        in_specs=prev_specs + [row3(D_MODEL)] + [_const_spec(c) for c in consts],
        out_specs=out_specs,
        scratch_shapes=[pltpu.VMEM((N_HEADS, slab), F32)],
        compiler_params=pltpu.CompilerParams(
            dimension_semantics=("arbitrary", "arbitrary"), vmem_limit_bytes=VMEM_LIMIT),
        name="in_proj_prompt",
    )(*stacked, x, *consts)


def _in_proj_sample(x, params, tm, seq_local):
    n = x.shape[0]
    bd = _in_proj_consts(tm)[0]
    g, w, bf, qg, kg, vg, wcat, sbias = params
    consts = (g, w, bf, qg, kg, vg, bd, wcat, sbias)
    row2 = lambda width: pl.BlockSpec((tm, width), lambda i: (i, 0))
    out_shape = (
        jax.ShapeDtypeStruct((n, WIDTH_A), F32),
        jax.ShapeDtypeStruct((n, WIDTH_A), F32),
        jax.ShapeDtypeStruct((n, WIDTH_A), F32),
        jax.ShapeDtypeStruct((n, N_HEADS), F32),
        jax.ShapeDtypeStruct((n, LANES), F32),
        jax.ShapeDtypeStruct((n, WIDTH_B), F32),
        jax.ShapeDtypeStruct((n, WIDTH_A), BF16),
        jax.ShapeDtypeStruct((n, WIDTH_B), BF16),
        jax.ShapeDtypeStruct((n, D_MODEL), BF16),
        jax.ShapeDtypeStruct((n, D_MODEL), BF16),
    )
    out_specs = (row2(WIDTH_A), row2(WIDTH_A), row2(WIDTH_A), row2(N_HEADS), row2(LANES), row2(WIDTH_B),
                 row2(WIDTH_A), row2(WIDTH_B), row2(D_MODEL), row2(D_MODEL))
    return pl.pallas_call(
        functools.partial(_in_proj_sample_kernel, seq_local=seq_local),
        out_shape=out_shape,
        grid=(n // tm,),
        in_specs=[row2(D_MODEL)] + [_const_spec(c) for c in consts],
        out_specs=out_specs,
        compiler_params=pltpu.CompilerParams(
            dimension_semantics=("arbitrary",), vmem_limit_bytes=VMEM_LIMIT),
        name="in_proj_sample",
    )(x, *consts)


def _attn_prompt_kernel(qt_ref, ka_ref, vt_ref, o_ref, s_ref, m_ref, al_ref, acc_ref):
    g = qt_ref.shape[1]
    tq = qt_ref.shape[-1]
    t = vt_ref.shape[-1]
    assert tq == 2 * t
    i = pl.program_id(2)
    key = lax.broadcasted_iota(jnp.int32, (t, tq), 0)
    qry = lax.broadcasted_iota(jnp.int32, (t, tq), 1)

    def scores(hd, blk, slot, mask, m_old):
        k = ka_ref[0, hd, pl.ds(pl.multiple_of(blk * t, t), t), :]
        s = _dot(k, qt_ref[0, hd])
        if mask is not None:
            s = jnp.where(mask, s, NEG_BIG)
        s_ref[slot, hd] = s
        m_new = jnp.maximum(m_old, jnp.max(s, axis=0, keepdims=True))
        m_ref[slot, hd, 0:1, :] = m_new
        al_ref[slot, hd, 0:1, :] = jnp.exp2(m_old - m_new)

    def iteration(j, cur, next_mask, has_next):
        for hd in range(g):
            p = jnp.exp2(s_ref[cur, hd] - m_ref[cur, hd, 0:1, :]).astype(BF16)
            if has_next:
                scores(hd, j + 1, 1 - cur, next_mask, m_ref[cur, hd, 0:1, :])
            acc_ref[hd] = al_ref[cur, hd, 0:1, :] * acc_ref[hd] + _dot(vt_ref[0, hd, j], p)

    acc_ref[...] = jnp.zeros_like(acc_ref)
    diag_lo = key <= qry
    diag_hi = key + t <= qry
    for hd in range(g):
        scores(hd, 0, 0, key <= qry + jnp.where(i > 0, tq, 0), jnp.full((1, tq), NEG_BIG, F32))

    def pair(jj, carry):
        iteration(2 * jj, 0, None, True)
        iteration(2 * jj + 1, 1, None, True)
        return carry

    lax.fori_loop(0, i - 1, pair, 0)

    @pl.when(i >= 1)
    def _():
        iteration(2 * i - 2, 0, None, True)
        iteration(2 * i - 1, 1, diag_lo, True)

    last = 2 * i + 1
    iteration(last - 1, 0, diag_hi, True)
    iteration(last, 1, None, False)
    for hd in range(g):
        acc = acc_ref[hd]
        o_ref[0, hd * HEAD_DIM:(hd + 1) * HEAD_DIM, :] = acc[0:HEAD_DIM] / acc[HEAD_DIM:HEAD_DIM + 1]


def _attn_prompt(qt, ka, vt, g):
    b, h, _, s = qt.shape
    t = vt.shape[-1]
    tq = 2 * t
    return pl.pallas_call(
        _attn_prompt_kernel,
        out_shape=jax.ShapeDtypeStruct((b, WIDTH_A, s), F32),
        grid=(b, h // g, s // tq),
        in_specs=[
            pl.BlockSpec((1, g, LANES, tq), lambda bi, hi, i: (bi, hi, 0, i)),
            pl.BlockSpec((1, g, s, LANES), lambda bi, hi, i: (bi, hi, 0, 0),
                         pipeline_mode=pl.Buffered(1)),
            pl.BlockSpec((1, g, s // t, VT_ROWS, t), lambda bi, hi, i: (bi, hi, 0, 0, 0),
                         pipeline_mode=pl.Buffered(1)),
        ],
        out_specs=pl.BlockSpec((1, g * HEAD_DIM, tq), lambda bi, hi, i: (bi, hi, i)),
        scratch_shapes=[pltpu.VMEM((2, g, t, tq), F32),
                        pltpu.VMEM((2, g, 8, tq), F32), pltpu.VMEM((2, g, 8, tq), F32),
                        pltpu.VMEM((g, VT_ROWS, tq), F32)],
        compiler_params=pltpu.CompilerParams(
            dimension_semantics=("arbitrary", "arbitrary", "arbitrary"), vmem_limit_bytes=VMEM_LIMIT),
        name="attn_prompt",
    )(qt, ka, vt)


def _attn_sample_one(q, k_new, v_new, logf_new, ls_ref, ones_ref, m3_ref, k_refs, v_refs, l_refs):
    t_new = q.shape[0]
    n_pages = len(k_refs)
    rows = t_new * N_HEADS

    def per_token(x):
        return jnp.concatenate(
            [jnp.broadcast_to(x[tk:tk + 1], (N_HEADS, x.shape[1])) for tk in range(t_new)], axis=0)

    row = lax.broadcasted_iota(jnp.int32, (rows, WIDTH_A), 0)
    lane = lax.broadcasted_iota(jnp.int32, (rows, WIDTH_A), 1)
    own_head = (row & (N_HEADS - 1)) == (lane >> HEAD_DIM_BITS)
    qbd = jnp.where(own_head, per_token(q), 0.0).astype(BF16)

    cn = logf_new * LOG2E
    trow = lax.broadcasted_iota(jnp.int32, (t_new, LANES), 0)
    sh = 1
    while sh < t_new:
        cn = cn + jnp.where(trow >= sh, pltpu.roll(cn, sh, axis=0), 0.0)
        sh *= 2
    row1 = lax.broadcasted_iota(jnp.int32, (rows, LANES), 0)
    lane1 = lax.broadcasted_iota(jnp.int32, (rows, LANES), 1)
    head_lane = (row1 & (N_HEADS - 1)) == lane1
    cn_col = jnp.sum(jnp.where(head_lane, per_token(cn), 0.0), axis=1, keepdims=True)

    x = jnp.concatenate([l_refs[p][...] for p in range(n_pages)], axis=0) * LOG2E
    xp = _split3(x)
    r_in = _dot3_right(xp, ls_ref[...])
    tot_b = _dot3_right(xp, ones_ref[...])
    r_page = _dot3_left(m3_ref[...], _split3(tot_b))
    r = r_in + r_page

    s_blocks = []
    for p in range(n_pages):
        kp_t = k_refs[p][...].astype(BF16)
        r_p = r[p * N_HEADS:(p + 1) * N_HEADS]
        s_blocks.append(_dot(qbd, kp_t) + jnp.concatenate([r_p] * t_new, axis=0) + cn_col)
    pad = jnp.zeros((PAGE - t_new, WIDTH_A), F32)
    kn = jnp.concatenate([k_new, pad], axis=0).astype(BF16)
    vn = jnp.concatenate([v_new, pad], axis=0).astype(BF16)
    cn_pad = jnp.concatenate([cn, jnp.zeros((PAGE - t_new, LANES), F32)], axis=0)
    sel = jnp.where(head_lane, 1.0, 0.0).astype(BF16)
    cn_keys = sum(_dot_nt(sel, part) for part in _split3(cn_pad))
    s_new = _dot_nt(qbd, kn) + cn_col - cn_keys
    s_new = jnp.where((lane1 <= (row1 >> HEAD_BITS)) & (lane1 < t_new), s_new, NEG_BIG)
    s_blocks.append(s_new)

    m = s_blocks[0]
    for sb in s_blocks[1:]:
        m = jnp.maximum(m, sb)
    m = jnp.max(m, axis=1, keepdims=True)
    l_sum = jnp.zeros((rows, 1), F32)
    o = jnp.zeros((rows, WIDTH_A), F32)
    for p in range(n_pages + 1):
        pe = jnp.exp2(s_blocks[p] - m)
        l_sum = l_sum + jnp.sum(pe, axis=1, keepdims=True)
        if p < n_pages:
            o = o + _dot_nt(pe.astype(BF16), v_refs[p][...].astype(BF16))
        else:
            o = o + _dot(pe.astype(BF16), vn)
    o = jnp.where(own_head, o / l_sum, 0.0)
    return jnp.concatenate(
        [jnp.sum(o[tk * N_HEADS:(tk + 1) * N_HEADS], axis=0, keepdims=True) for tk in range(t_new)], axis=0)


def _attn_sample_consts(n_pages):
    key = np.arange(PAGE)
    ls = (key[:, None] > key[None, :]).astype(np.float32)
    ones = np.ones((PAGE, LANES), np.float32)
    ph = np.arange(n_pages * N_HEADS)
    m3 = ((ph[:, None] % N_HEADS == ph[None, :] % N_HEADS)
          & (ph[None, :] // N_HEADS > ph[:, None] // N_HEADS)).astype(np.float32)
    return tuple(jnp.asarray(a, BF16) for a in (ls, ones, m3))


def _merge_kernel(transposed, slab, a_ref, ga_ref, bm_ref, sa_ref, sb_ref, x_ref, wpa_ref, wpb_ref, wo_ref,
                  y_ref):
    for sub in range(x_ref.shape[-2] // slab):
        rows = slice(sub * slab, (sub + 1) * slab)
        at = (lambda ref: ref[0, rows]) if transposed else (lambda ref: ref[rows])
        attn = a_ref[0, :, rows].T if transposed else a_ref[rows]
        a = (attn * at(ga_ref).astype(F32)).astype(BF16)
        pa = _dot(a, wpa_ref[...])
        pb = _dot(at(bm_ref), wpb_ref[...])
        m = at(sa_ref).astype(F32) * pa + at(sb_ref).astype(F32) * pb
        y = at(x_ref) + _dot(m.astype(BF16), wo_ref[...])
        if transposed:
            y_ref[0, rows] = y
        else:
            y_ref[rows] = y


def _merge_attn_kernel(layer, n_pages, t_new, seqs, pt_ref, a_ref, ga_ref, bm_ref, sa_ref, sb_ref, x_ref,
                       wpa_ref, wpb_ref, wo_ref, q_ref, kn_ref, vn_ref, ln_ref, ls_ref, ones_ref, m3_ref,
                       ck_hbm, cv_hbm, cl_hbm, y_ref, o_ref, kbuf, vbuf, lbuf, sems):
    nt = pl.num_programs(1)
    st = pl.program_id(0) * nt + pl.program_id(1)
    last = pl.num_programs(0) * nt - 1
    slot = st & 1
    per_step = seqs * n_pages
    kinds = ((ck_hbm, kbuf), (cv_hbm, vbuf), (cl_hbm, lbuf))

    def page_copy(kind, page, to_slot, idx):
        hbm, buf = kinds[kind]
        return pltpu.make_async_copy(hbm.at[layer, page], buf.at[to_slot, idx], sems.at[to_slot, kind])

    def fetch(step, to_slot):
        for idx in range(per_step):
            page = pt_ref[step * seqs + idx // n_pages, idx % n_pages]
            for kind in range(3):
                page_copy(kind, page, to_slot, idx).start()

    @pl.when(st == 0)
    def _():
        fetch(0, 0)

    @pl.when(st < last)
    def _():
        fetch(st + 1, 1 - slot)

    _merge_kernel(True, x_ref.shape[-2], a_ref, ga_ref, bm_ref, sa_ref, sb_ref, x_ref, wpa_ref, wpb_ref,
                  wo_ref, y_ref)

    for idx in range(per_step):
        for kind in range(3):
            page_copy(kind, 0, slot, idx).wait()
    for u in range(seqs):
        tok = slice(u * t_new, (u + 1) * t_new)
        pages = [[buf.at[slot, u * n_pages + p] for p in range(n_pages)] for _, buf in kinds]
        o_ref[tok] = _attn_sample_one(q_ref[tok], kn_ref[tok], vn_ref[tok], ln_ref[tok],
                                      ls_ref, ones_ref, m3_ref, *pages)


def _merge_prompt_attn_sample(layer, page_table, attn_t, ga, bm, sa, sb, x, wpa, wpb, wo, q, kn, vn, lnp,
                              ck, cv, clt, t_new, seqs):
    b, s, _ = x.shape
    n_seq, n_pages = page_table.shape
    nt = n_seq // (seqs * b)
    tm = s // nt
    assert b * nt * seqs == n_seq and nt * tm == s
    consts = _attn_sample_consts(n_pages)
    step = lambda bi, i: bi * nt + i
    row3 = lambda width: pl.BlockSpec((1, tm, width), lambda bi, i, pt: (bi, i, 0))
    tok = lambda width: pl.BlockSpec((seqs * t_new, width), lambda bi, i, pt: (step(bi, i), 0))

    per_step = seqs * n_pages
    grid_spec = pltpu.PrefetchScalarGridSpec(
        num_scalar_prefetch=1,
        grid=(b, nt),
        in_specs=[pl.BlockSpec((1, WIDTH_A, tm), lambda bi, i, pt: (bi, 0, i)),
                  row3(WIDTH_A), row3(WIDTH_B), row3(D_MODEL), row3(D_MODEL), row3(D_MODEL),
                  _const_spec(wpa), _const_spec(wpb), _const_spec(wo),
                  tok(WIDTH_A), tok(WIDTH_A), tok(WIDTH_A), tok(LANES)]
        + [_const_spec(c) for c in consts]
        + [pl.BlockSpec(memory_space=pl.ANY)] * 3,
        out_specs=(row3(D_MODEL), tok(WIDTH_A)),
        scratch_shapes=[pltpu.VMEM((2, per_step, WIDTH_A, PAGE), F32),
                        pltpu.VMEM((2, per_step, WIDTH_A, PAGE), F32),
                        pltpu.VMEM((2, per_step, N_HEADS, PAGE), F32),
                        pltpu.SemaphoreType.DMA((2, 3))],
    )
    return pl.pallas_call(
        functools.partial(_merge_attn_kernel, layer, n_pages, t_new, seqs),
        out_shape=(jax.ShapeDtypeStruct((b, s, D_MODEL), F32),
                   jax.ShapeDtypeStruct((n_seq * t_new, WIDTH_A), F32)),
        grid_spec=grid_spec,
        compiler_params=pltpu.CompilerParams(
            dimension_semantics=("arbitrary", "arbitrary"), vmem_limit_bytes=VMEM_LIMIT),
        name="merge_prompt_attn_sample",
    )(page_table, attn_t, ga, bm, sa, sb, x, wpa, wpb, wo, q, kn, vn, lnp, *consts, ck, cv, clt)


def _merge_sample(attn, ga, bm, sa, sb, x, wpa, wpb, wo, tm):
    n = x.shape[0]
    row2 = lambda width: pl.BlockSpec((tm, width), lambda i: (i, 0))
    return pl.pallas_call(
        functools.partial(_merge_kernel, False, tm),
        out_shape=jax.ShapeDtypeStruct((n, D_MODEL), F32),
        grid=(n // tm,),
        in_specs=[row2(WIDTH_A), row2(WIDTH_A), row2(WIDTH_B), row2(D_MODEL), row2(D_MODEL), row2(D_MODEL),
                  _const_spec(wpa), _const_spec(wpb), _const_spec(wo)],
        out_specs=row2(D_MODEL),
        compiler_params=pltpu.CompilerParams(
            dimension_semantics=("arbitrary",), vmem_limit_bytes=VMEM_LIMIT),
        name="merge_sample",
    )(attn, ga, bm, sa, sb, x, wpa, wpb, wo)


KEY_BLOCK = 256
TM_PROMPT = 512
HEADS_PER_STEP = 8
SEQS_PER_STEP = 2
TM_SAMPLE = 256


def kernel(x_prompt, x_sample, cache_k, cache_v, cache_logf, page_table, norm_g, w_in, b_forget, q_norm_g,
           k_norm_g, v_norm_g, w_s, b_s, w_proj_a, w_proj_b, w_out):
    depth = w_in.shape[0]
    bsz, seq, _ = x_prompt.shape
    n_seq, t_new, _ = x_sample.shape
    n_pool = cache_k.shape[1]

    f_lo = 4 * WIDTH_A
    w_head = w_in[:, :, :f_lo].astype(BF16)
    w_tail = w_in[:, :, f_lo + N_HEADS:].astype(BF16)
    w_forget = jnp.pad(w_in[:, :, f_lo:f_lo + N_HEADS],
                       ((0, 0), (0, 0), (0, C_END - C_F - N_HEADS))).astype(BF16)
    bf_pad = jnp.pad(b_forget, ((0, 0), (0, LANES - N_HEADS)))[:, None, :]
    qg = jnp.tile(q_norm_g, (1, N_HEADS))[:, None, :]
    kg = jnp.tile(k_norm_g, (1, N_HEADS))[:, None, :]
    gain_cols = lambda gn: jnp.tile(gn[:, :, None], (1, 2, KEY_BLOCK))
    qg_t = gain_cols(q_norm_g * (HEAD_DIM ** -0.5 * LOG2E))
    kg_t = gain_cols(k_norm_g)
    vg = v_norm_g[:, None, :]
    ng = norm_g[:, None, :]
    wcat_p = w_s.transpose(0, 2, 1, 3).reshape(depth, CHUNK, N_GROUPS * CHUNK).astype(BF16)
    w_small = jnp.tile(w_s[:, :, :t_new, :t_new], (1, 1, CHUNK // t_new, CHUNK // t_new))
    wcat_s = w_small.transpose(0, 2, 1, 3).reshape(depth, CHUNK, N_GROUPS * CHUNK).astype(BF16)
    sbias_p = jnp.repeat(b_s.transpose(0, 2, 1), GROUP_DIM, axis=-1)
    sbias_s = jnp.repeat(jnp.tile(b_s[:, :, :t_new], (1, 1, CHUNK // t_new)).transpose(0, 2, 1),
                         GROUP_DIM, axis=-1)
    wpa = w_proj_a.astype(BF16)
    wpb = w_proj_b.astype(BF16)
    wo = w_out.astype(BF16)
    ck = cache_k.transpose(0, 1, 3, 4, 2).reshape(depth, n_pool, WIDTH_A, PAGE)
    cv = cache_v.transpose(0, 1, 3, 4, 2).reshape(depth, n_pool, WIDTH_A, PAGE)
    clt = cache_logf.transpose(0, 1, 3, 2)

    xp = x_prompt
    xs = x_sample.reshape(n_seq * t_new, D_MODEL)
    outs = [[] for _ in range(4)]
    stacked = None
    for l in range(depth):
        weights = (w_head[l], w_tail[l], w_forget[l])
        qt, ka, vt, *stacked, ga, bm, sa, sb = _in_proj_prompt(
            xp, (ng[l], weights, bf_pad[l], qg_t[l], kg_t[l], vg[l], wcat_p[l], sbias_p[l]),
            TM_PROMPT, KEY_BLOCK, l, stacked)
        common = (ng[l], weights, bf_pad[l], qg[l], kg[l], vg[l])
        attn_t = _attn_prompt(qt, ka, vt, HEADS_PER_STEP)

        q_s, k_s, v_s, lf_s, lfp_s, vbn_s, ga_s, bm_s, sa_s, sb_s = _in_proj_sample(
            xs, common + (wcat_s[l], sbias_s[l]), TM_SAMPLE, t_new)
        xp, attn_s = _merge_prompt_attn_sample(
            l, page_table, attn_t, ga, bm, sa, sb, xp, wpa[l], wpb[l], wo[l], q_s, k_s, v_s, lfp_s,
            ck, cv, clt, t_new, SEQS_PER_STEP)
        xs = _merge_sample(attn_s, ga_s, bm_s, sa_s, sb_s, xs, wpa[l], wpb[l], wo[l], TM_SAMPLE)
        outs[0].append(k_s.reshape(n_seq, t_new, N_HEADS, HEAD_DIM))
        outs[1].append(v_s.reshape(n_seq, t_new, N_HEADS, HEAD_DIM))
        outs[2].append(lf_s.reshape(n_seq, t_new, N_HEADS))
        outs[3].append(vbn_s.reshape(n_seq, t_new, WIDTH_B))
    kt_all, vt_all, lft_all = stacked
    unstack = lambda a: a.reshape(depth, bsz, N_HEADS, HEAD_DIM, seq).transpose(0, 1, 4, 2, 3)
    return ((xp, xs.reshape(n_seq, t_new, D_MODEL), unstack(kt_all), unstack(vt_all),
             lft_all.transpose(0, 1, 3, 2)) + tuple(jnp.stack(o) for o in outs))
```

```python
import functools
import math

import numpy as np
import jax
import jax.numpy as jnp
from jax import lax
from jax.experimental import pallas as pl
from jax.experimental.pallas import tpu as pltpu

D_MODEL = 1024
N_HEADS = 8
HEAD_DIM = 64
HEAD_BITS = 3
HEAD_DIM_BITS = 6
WIDTH_A = N_HEADS * HEAD_DIM
N_GROUPS = 8
GROUP_DIM = 64
GROUP_DIM_BITS = 6
WIDTH_B = N_GROUPS * GROUP_DIM
CHUNK = 128
PAGE = 128
EPS = 1e-6
LOG2E = math.log2(math.e)
NEG_BIG = -1e30

LANES = 128
VT_ROWS = 80
VMEM_LIMIT = 56 * 1024 * 1024

C_Q, C_K, C_V, C_GA, C_U, C_VB, C_GB, C_SA, C_SB, C_F, C_END = (
    0, 512, 1024, 1536, 2048, 2560, 3072, 3584, 4608, 5632, 5760)

F32 = jnp.float32
BF16 = jnp.bfloat16


def _split3(x):
    hi = x.astype(BF16)
    r1 = x - hi.astype(F32)
    mid = r1.astype(BF16)
    lo = (r1 - mid.astype(F32)).astype(BF16)
    return hi, mid, lo


def _dot(a, b):
    return jnp.dot(a, b, preferred_element_type=F32)


def _dot_nt(a, b):
    return lax.dot_general(a, b, (((1,), (1,)), ((), ())), preferred_element_type=F32)


def _dot3_right(parts, w):
    return _dot(parts[0], w) + _dot(parts[1], w) + _dot(parts[2], w)


def _dot3_left(w, parts):
    return _dot(w, parts[0]) + _dot(w, parts[1]) + _dot(w, parts[2])


def _sigmoid(x):
    return 1.0 / (1.0 + jnp.exp(-x))


def _gelu_tanh(x):
    c = math.sqrt(2.0 / math.pi)
    return 0.5 * x * (1.0 + jnp.tanh(c * (x + 0.044715 * (x * x * x))))


def _log_sigmoid(x):
    return jnp.minimum(x, 0.0) - jnp.log1p(jnp.exp(-jnp.abs(x)))


def _in_proj_common(x, g_ref, w_ref, bf_ref, qg_ref, kg_ref, vg_ref, bd_ref, wcat_ref, sbias_ref,
                    seq_local):
    tm = x.shape[0]
    ms = jnp.mean(x * x, axis=-1, keepdims=True)
    h = ((x * lax.rsqrt(ms + EPS)) * g_ref[...]).astype(BF16)

    def seg(lo, hi):
        w_head, w_tail, w_forget = w_ref
        if hi <= C_U:
            return _dot(h, w_head[:, lo:hi])
        if lo >= C_F:
            return _dot(h, w_forget[...])
        return _dot(h, w_tail[:, lo - C_U:hi - C_U])

    def head_norm(z, gain):
        msz = _dot((z * z).astype(BF16), bd_ref[...]) * (1.0 / HEAD_DIM)
        return z * lax.rsqrt(msz + EPS) * gain

    if bd_ref is None:
        qn, kn = seg(C_Q, C_K), seg(C_K, C_V)
    else:
        qn = head_norm(seg(C_Q, C_K), qg_ref[...])
        kn = head_norm(seg(C_K, C_V), kg_ref[...])
    v = seg(C_V, C_GA)
    zga = seg(C_GA, C_U)
    ga = zga * _sigmoid(zga)

    lane = lax.broadcasted_iota(jnp.int32, (tm, LANES), 1)
    logf = jnp.where(lane < N_HEADS, _log_sigmoid(seg(C_F, C_END) + bf_ref[...]), 0.0)

    u = _gelu_tanh(seg(C_U, C_VB))
    vbg = _gelu_tanh(seg(C_VB, C_GB))
    msv = jnp.mean(vbg * vbg, axis=-1, keepdims=True)
    vbn = vbg * lax.rsqrt(msv + EPS) * vg_ref[...]
    zgb = seg(C_GB, C_SA)
    ugb = u * (zgb * _sigmoid(zgb))

    t_idx = lax.broadcasted_iota(jnp.int32, (CHUNK, N_GROUPS * CHUNK), 0)
    s_idx = lax.broadcasted_iota(jnp.int32, (CHUNK, N_GROUPS * CHUNK), 1) & (CHUNK - 1)
    keep = s_idx <= t_idx
    if seq_local < CHUNK:
        shift = int(math.log2(seq_local))
        keep = keep & ((s_idx >> shift) == (t_idx >> shift))
    wm = jnp.where(keep, wcat_ref[...], jnp.zeros((), BF16))
    gq = N_GROUPS // 2
    glane = lax.broadcasted_iota(jnp.int32, (CHUNK, gq * GROUP_DIM), 1) >> GROUP_DIM_BITS
    vbn_b = vbn.astype(BF16)
    s_chunks = []
    for c in range(tm // CHUNK):
        halves = []
        for hf in range(2):
            vc = vbn_b[c * CHUNK:(c + 1) * CHUNK, hf * gq * GROUP_DIM:(hf + 1) * gq * GROUP_DIM]
            rhs = jnp.concatenate(
                [jnp.where(glane == g, vc, jnp.zeros((), BF16)) for g in range(gq)], axis=0)
            halves.append(_dot(wm[:, hf * gq * CHUNK:(hf + 1) * gq * CHUNK], rhs))
        s_chunks.append(jnp.concatenate(halves, axis=1) + sbias_ref[...])
    s_sgu = jnp.concatenate(s_chunks, axis=0)
    bmix = ugb * s_sgu

    sa = _sigmoid(seg(C_SA, C_SB))
    sb = _sigmoid(seg(C_SB, C_F))
    return qn, kn, v, logf, vbn, ga, bmix, sa, sb


def _in_proj_prompt_kernel(layer, *refs):
    n_prev = 3 if layer else 0
    (x_ref, g_ref, w_ref, bf_ref, qgt_ref, kgt_ref, vg_ref, wcat_ref, sbias_ref, tri_ref, pk_ref,
     pq_ref, qt_ref, ka_ref, vt_ref, ko_ref, vo_ref, lf_ref, ga_ref, bm_ref, sa_ref, sb_ref,
     carry_ref) = refs[n_prev:]
    for prev_ref, out_ref in zip(refs[:n_prev], (ko_ref, vo_ref, lf_ref)):
        out_ref[0:layer] = prev_ref[...]
    sr = vt_ref.shape[-1]

    @pl.when(pl.program_id(1) == 0)
    def _():
        carry_ref[...] = jnp.zeros_like(carry_ref)

    def head_norm_t(zt, gain_t):
        sq = zt * zt
        parts = []
        for hh in range(2):
            ms = jnp.sum(sq[hh * HEAD_DIM:(hh + 1) * HEAD_DIM], axis=0, keepdims=True) * (1.0 / HEAD_DIM)
            parts.append(jnp.broadcast_to(lax.rsqrt(ms + EPS), (HEAD_DIM, sr)))
        return zt * jnp.concatenate(parts, axis=0) * gain_t

    carry = carry_ref[...]
    lane = lax.broadcasted_iota(jnp.int32, (sr, LANES), 1)
    row8 = lax.broadcasted_iota(jnp.int32, (N_HEADS, sr), 0)
    pad_rows = jnp.zeros((LANES - N_HEADS, sr), F32)
    ones_blk = jnp.where(lax.broadcasted_iota(jnp.int32, (VT_ROWS - HEAD_DIM, sr), 0) == 0, 1.0, 0.0)
    for sub in range(x_ref.shape[-2] // sr):
        rows = slice(sub * sr, (sub + 1) * sr)
        zq, zk, v, logf, _, ga, bmix, sa, sb = _in_proj_common(
            x_ref[0, rows], g_ref, w_ref, bf_ref, None, None, vg_ref, None, wcat_ref, sbias_ref, CHUNK)

        lft = logf.T[0:N_HEADS]
        lf_ref[layer, 0, :, rows] = lft
        ga_ref[0, rows] = ga.astype(BF16)
        bm_ref[0, rows] = bmix.astype(BF16)
        sa_ref[0, rows] = sa.astype(BF16)
        sb_ref[0, rows] = sb.astype(BF16)

        stacked = jnp.concatenate([part.astype(F32) for part in _split3(lft)]
                                  + [jnp.zeros((N_HEADS, sr), F32)], axis=0).astype(BF16)
        cs = _dot(stacked, tri_ref[...])
        ct = cs[0:N_HEADS] + cs[N_HEADS:2 * N_HEADS] + cs[2 * N_HEADS:3 * N_HEADS] + carry
        carry = jnp.broadcast_to(ct[:, sr - 1:sr], (N_HEADS, sr))
        c2t = ct * LOG2E
        c2 = jnp.concatenate([c2t, pad_rows], axis=0).T

        hi, mid, lo = _split3(c2)
        cp = (hi.astype(F32) + pltpu.roll(mid.astype(F32), 8, axis=1)
              + pltpu.roll(lo.astype(F32), 16, axis=1) + jnp.where(lane == 24, 1.0, 0.0))
        extra_k = _dot(cp.astype(BF16), pk_ref[...])

        hit, midt, lot = (part.astype(F32) for part in _split3(c2t))
        cpt = jnp.concatenate(
            [hit, midt, lot, jnp.where(row8 == 0, 1.0, 0.0), jnp.zeros((LANES - 32, sr), F32)],
            axis=0).astype(BF16)
        extra_q = _dot(pq_ref[...], cpt)

        for j in range(N_HEADS // 2):
            qs_t = head_norm_t(zq[:, j * LANES:(j + 1) * LANES].T, qgt_ref[...])
            ks_t = head_norm_t(zk[:, j * LANES:(j + 1) * LANES].T, kgt_ref[...])
            vs_t = v[:, j * LANES:(j + 1) * LANES].T
            ks = ks_t.T
            ks_sw = pltpu.roll(ks, HEAD_DIM, axis=1)
            ko_ref[layer, 0, j * LANES:(j + 1) * LANES, rows] = ks_t
            vo_ref[layer, 0, j * LANES:(j + 1) * LANES, rows] = vs_t
            for hh in range(2):
                hd = 2 * j + hh
                qt_ref[0, hd, :, rows] = jnp.concatenate(
                    [qs_t[hh * HEAD_DIM:(hh + 1) * HEAD_DIM], extra_q[hd * HEAD_DIM:(hd + 1) * HEAD_DIM]],
                    axis=0).astype(BF16)
                vt_ref[0, hd, sub] = jnp.concatenate(
                    [vs_t[hh * HEAD_DIM:(hh + 1) * HEAD_DIM], ones_blk], axis=0).astype(BF16)
                ka_ref[0, hd, rows] = jnp.where(lane < HEAD_DIM, ks if hh == 0 else ks_sw,
                                                extra_k).astype(BF16)
    carry_ref[...] = carry


def _in_proj_sample_kernel(x_ref, g_ref, w_ref, bf_ref, qg_ref, kg_ref, vg_ref, bd_ref, wcat_ref,
                           sbias_ref,
                           q_ref, ko_ref, vo_ref, lf_ref, lfp_ref, vbn_ref, ga_ref, bm_ref, sa_ref, sb_ref,
                           *, seq_local):
    qn, kn, v, logf, vbn, ga, bmix, sa, sb = _in_proj_common(
        x_ref[...], g_ref, w_ref, bf_ref, qg_ref, kg_ref, vg_ref, bd_ref, wcat_ref, sbias_ref, seq_local)
    q_ref[...] = qn * (HEAD_DIM ** -0.5 * LOG2E)
    ko_ref[...] = kn
    vo_ref[...] = v
    lf_ref[...] = logf[:, :N_HEADS]
    lfp_ref[...] = logf
    vbn_ref[...] = vbn
    ga_ref[...] = ga.astype(BF16)
    bm_ref[...] = bmix.astype(BF16)
    sa_ref[...] = sa.astype(BF16)
    sb_ref[...] = sb.astype(BF16)


def _const_spec(arr):
    if isinstance(arr, tuple):
        return tuple(_const_spec(a) for a in arr)
    return pl.BlockSpec(arr.shape, lambda *_: (0,) * arr.ndim, pipeline_mode=pl.Buffered(1))


def _in_proj_consts(tm):
    bd = np.kron(np.eye(N_HEADS, dtype=np.float32), np.ones((HEAD_DIM, HEAD_DIM), np.float32))
    tri = np.triu(np.ones((tm, tm), np.float32))
    pk = np.zeros((LANES, LANES), np.float32)
    pq = np.zeros((N_HEADS * HEAD_DIM, LANES), np.float32)
    for hd in range(N_HEADS):
        for part in range(3):
            pk[8 * part + hd, HEAD_DIM + 3 * hd + part] = -1.0
            pk[24, HEAD_DIM + 3 * N_HEADS + part] = 1.0
            pq[hd * HEAD_DIM + 3 * hd + part, 24] = 1.0
            pq[hd * HEAD_DIM + 3 * N_HEADS + part, 8 * part + hd] = 1.0
    return (jnp.asarray(bd, BF16), jnp.asarray(tri, BF16), jnp.asarray(pk, BF16), jnp.asarray(pq, BF16))


def _in_proj_prompt(x, params, tm, slab, layer, stacked):
    b, s, _ = x.shape
    nt = s // tm
    _, tri, pk, pq = _in_proj_consts(slab)
    g, w, bf, qgt, kgt, vg, wcat, sbias = params
    consts = (g, w, bf, qgt, kgt, vg, wcat, sbias, tri, pk, pq)
    stacked = () if stacked is None else tuple(stacked)
    assert len(stacked) == (3 if layer else 0)
    row3 = lambda width: pl.BlockSpec((1, tm, width), lambda bi, i: (bi, i, 0))
    col4 = lambda n, rows: pl.BlockSpec((n, 1, rows, tm), lambda bi, i: (0, bi, 0, i))
    depth = layer + 1
    out_shape = (
        jax.ShapeDtypeStruct((b, N_HEADS, LANES, s), BF16),
        jax.ShapeDtypeStruct((b, N_HEADS, s, LANES), BF16),
        jax.ShapeDtypeStruct((b, N_HEADS, s // slab, VT_ROWS, slab), BF16),
        jax.ShapeDtypeStruct((depth, b, WIDTH_A, s), F32),
        jax.ShapeDtypeStruct((depth, b, WIDTH_A, s), F32),
        jax.ShapeDtypeStruct((depth, b, N_HEADS, s), F32),
        jax.ShapeDtypeStruct((b, s, WIDTH_A), BF16),
        jax.ShapeDtypeStruct((b, s, WIDTH_B), BF16),
        jax.ShapeDtypeStruct((b, s, D_MODEL), BF16),
        jax.ShapeDtypeStruct((b, s, D_MODEL), BF16),
    )
    out_specs = (
        pl.BlockSpec((1, N_HEADS, LANES, tm), lambda bi, i: (bi, 0, 0, i)),
        pl.BlockSpec((1, N_HEADS, tm, LANES), lambda bi, i: (bi, 0, i, 0)),
        pl.BlockSpec((1, N_HEADS, tm // slab, VT_ROWS, slab), lambda bi, i: (bi, 0, i, 0, 0)),
        col4(depth, WIDTH_A), col4(depth, WIDTH_A), col4(depth, N_HEADS), row3(WIDTH_A), row3(WIDTH_B),
        row3(D_MODEL), row3(D_MODEL),
    )
    prev_specs = [col4(layer, WIDTH_A), col4(layer, WIDTH_A), col4(layer, N_HEADS)] if layer else []
    return pl.pallas_call(
        functools.partial(_in_proj_prompt_kernel, layer),
        out_shape=out_shape,
        grid=(b, nt),
        in_specs=prev_specs + [row3(D_MODEL)] + [_const_spec(c) for c in consts],
        out_specs=out_specs,
        scratch_shapes=[pltpu.VMEM((N_HEADS, slab), F32)],
        compiler_params=pltpu.CompilerParams(
            dimension_semantics=("arbitrary", "arbitrary"), vmem_limit_bytes=VMEM_LIMIT),
        name="in_proj_prompt",
    )(*stacked, x, *consts)


def _in_proj_sample(x, params, tm, seq_local):
    n = x.shape[0]
    bd = _in_proj_consts(tm)[0]
    g, w, bf, qg, kg, vg, wcat, sbias = params
    consts = (g, w, bf, qg, kg, vg, bd, wcat, sbias)
    row2 = lambda width: pl.BlockSpec((tm, width), lambda i: (i, 0))
    out_shape = (
        jax.ShapeDtypeStruct((n, WIDTH_A), F32),
        jax.ShapeDtypeStruct((n, WIDTH_A), F32),
        jax.ShapeDtypeStruct((n, WIDTH_A), F32),
        jax.ShapeDtypeStruct((n, N_HEADS), F32),
        jax.ShapeDtypeStruct((n, LANES), F32),
        jax.ShapeDtypeStruct((n, WIDTH_B), F32),
        jax.ShapeDtypeStruct((n, WIDTH_A), BF16),
        jax.ShapeDtypeStruct((n, WIDTH_B), BF16),
        jax.ShapeDtypeStruct((n, D_MODEL), BF16),
        jax.ShapeDtypeStruct((n, D_MODEL), BF16),
    )
    out_specs = (row2(WIDTH_A), row2(WIDTH_A), row2(WIDTH_A), row2(N_HEADS), row2(LANES), row2(WIDTH_B),
                 row2(WIDTH_A), row2(WIDTH_B), row2(D_MODEL), row2(D_MODEL))
    return pl.pallas_call(
        functools.partial(_in_proj_sample_kernel, seq_local=seq_local),
        out_shape=out_shape,
        grid=(n // tm,),
        in_specs=[row2(D_MODEL)] + [_const_spec(c) for c in consts],
        out_specs=out_specs,
        compiler_params=pltpu.CompilerParams(
            dimension_semantics=("arbitrary",), vmem_limit_bytes=VMEM_LIMIT),
        name="in_proj_sample",
    )(x, *consts)


def _attn_prompt_kernel(qt_ref, ka_ref, vt_ref, o_ref, s_ref, m_ref, al_ref, acc_ref):
    g = qt_ref.shape[1]
    tq = qt_ref.shape[-1]
    t = vt_ref.shape[-1]
    assert tq == 2 * t
    i = pl.program_id(2)
    key = lax.broadcasted_iota(jnp.int32, (t, tq), 0)
    qry = lax.broadcasted_iota(jnp.int32, (t, tq), 1)

    def scores(hd, blk, slot, mask, m_old):
        k = ka_ref[0, hd, pl.ds(pl.multiple_of(blk * t, t), t), :]
        s = _dot(k, qt_ref[0, hd])
        if mask is not None:
            s = jnp.where(mask, s, NEG_BIG)
        s_ref[slot, hd] = s
        m_new = jnp.maximum(m_old, jnp.max(s, axis=0, keepdims=True))
        m_ref[slot, hd, 0:1, :] = m_new
        al_ref[slot, hd, 0:1, :] = jnp.exp2(m_old - m_new)

    def iteration(j, cur, next_mask, has_next):
        for hd in range(g):
            p = jnp.exp2(s_ref[cur, hd] - m_ref[cur, hd, 0:1, :]).astype(BF16)
            if has_next:
                scores(hd, j + 1, 1 - cur, next_mask, m_ref[cur, hd, 0:1, :])
            acc_ref[hd] = al_ref[cur, hd, 0:1, :] * acc_ref[hd] + _dot(vt_ref[0, hd, j], p)

    acc_ref[...] = jnp.zeros_like(acc_ref)
    diag_lo = key <= qry
    diag_hi = key + t <= qry
    for hd in range(g):
        scores(hd, 0, 0, key <= qry + jnp.where(i > 0, tq, 0), jnp.full((1, tq), NEG_BIG, F32))

    def pair(jj, carry):
        iteration(2 * jj, 0, None, True)
        iteration(2 * jj + 1, 1, None, True)
        return carry

    lax.fori_loop(0, i - 1, pair, 0)

    @pl.when(i >= 1)
    def _():
        iteration(2 * i - 2, 0, None, True)
        iteration(2 * i - 1, 1, diag_lo, True)

    last = 2 * i + 1
    iteration(last - 1, 0, diag_hi, True)
    iteration(last, 1, None, False)
    for hd in range(g):
        acc = acc_ref[hd]
        o_ref[0, hd * HEAD_DIM:(hd + 1) * HEAD_DIM, :] = acc[0:HEAD_DIM] / acc[HEAD_DIM:HEAD_DIM + 1]


def _attn_prompt(qt, ka, vt, g):
    b, h, _, s = qt.shape
    t = vt.shape[-1]
    tq = 2 * t
    return pl.pallas_call(
        _attn_prompt_kernel,
        out_shape=jax.ShapeDtypeStruct((b, WIDTH_A, s), F32),
        grid=(b, h // g, s // tq),
        in_specs=[
            pl.BlockSpec((1, g, LANES, tq), lambda bi, hi, i: (bi, hi, 0, i)),
            pl.BlockSpec((1, g, s, LANES), lambda bi, hi, i: (bi, hi, 0, 0),
                         pipeline_mode=pl.Buffered(1)),
            pl.BlockSpec((1, g, s // t, VT_ROWS, t), lambda bi, hi, i: (bi, hi, 0, 0, 0),
                         pipeline_mode=pl.Buffered(1)),
        ],
        out_specs=pl.BlockSpec((1, g * HEAD_DIM, tq), lambda bi, hi, i: (bi, hi, i)),
        scratch_shapes=[pltpu.VMEM((2, g, t, tq), F32),
                        pltpu.VMEM((2, g, 8, tq), F32), pltpu.VMEM((2, g, 8, tq), F32),
                        pltpu.VMEM((g, VT_ROWS, tq), F32)],
        compiler_params=pltpu.CompilerParams(
            dimension_semantics=("arbitrary", "arbitrary", "arbitrary"), vmem_limit_bytes=VMEM_LIMIT),
        name="attn_prompt",
    )(qt, ka, vt)


def _attn_sample_one(q, k_new, v_new, logf_new, ls_ref, ones_ref, m3_ref, k_refs, v_refs, l_refs):
    t_new = q.shape[0]
    n_pages = len(k_refs)
    rows = t_new * N_HEADS

    def per_token(x):
        return jnp.concatenate(
            [jnp.broadcast_to(x[tk:tk + 1], (N_HEADS, x.shape[1])) for tk in range(t_new)], axis=0)

    row = lax.broadcasted_iota(jnp.int32, (rows, WIDTH_A), 0)
    lane = lax.broadcasted_iota(jnp.int32, (rows, WIDTH_A), 1)
    own_head = (row & (N_HEADS - 1)) == (lane >> HEAD_DIM_BITS)
    qbd = jnp.where(own_head, per_token(q), 0.0).astype(BF16)

    cn = logf_new * LOG2E
    trow = lax.broadcasted_iota(jnp.int32, (t_new, LANES), 0)
    sh = 1
    while sh < t_new:
        cn = cn + jnp.where(trow >= sh, pltpu.roll(cn, sh, axis=0), 0.0)
        sh *= 2
    row1 = lax.broadcasted_iota(jnp.int32, (rows, LANES), 0)
    lane1 = lax.broadcasted_iota(jnp.int32, (rows, LANES), 1)
    head_lane = (row1 & (N_HEADS - 1)) == lane1
    cn_col = jnp.sum(jnp.where(head_lane, per_token(cn), 0.0), axis=1, keepdims=True)

    x = jnp.concatenate([l_refs[p][...] for p in range(n_pages)], axis=0) * LOG2E
    xp = _split3(x)
    r_in = _dot3_right(xp, ls_ref[...])
    tot_b = _dot3_right(xp, ones_ref[...])
    r_page = _dot3_left(m3_ref[...], _split3(tot_b))
    r = r_in + r_page

    s_blocks = []
    for p in range(n_pages):
        kp_t = k_refs[p][...].astype(BF16)
        r_p = r[p * N_HEADS:(p + 1) * N_HEADS]
        s_blocks.append(_dot(qbd, kp_t) + jnp.concatenate([r_p] * t_new, axis=0) + cn_col)
    pad = jnp.zeros((PAGE - t_new, WIDTH_A), F32)
    kn = jnp.concatenate([k_new, pad], axis=0).astype(BF16)
    vn = jnp.concatenate([v_new, pad], axis=0).astype(BF16)
    cn_pad = jnp.concatenate([cn, jnp.zeros((PAGE - t_new, LANES), F32)], axis=0)
    sel = jnp.where(head_lane, 1.0, 0.0).astype(BF16)
    cn_keys = sum(_dot_nt(sel, part) for part in _split3(cn_pad))
    s_new = _dot_nt(qbd, kn) + cn_col - cn_keys
    s_new = jnp.where((lane1 <= (row1 >> HEAD_BITS)) & (lane1 < t_new), s_new, NEG_BIG)
    s_blocks.append(s_new)

    m = s_blocks[0]
    for sb in s_blocks[1:]:
        m = jnp.maximum(m, sb)
    m = jnp.max(m, axis=1, keepdims=True)
    l_sum = jnp.zeros((rows, 1), F32)
    o = jnp.zeros((rows, WIDTH_A), F32)
    for p in range(n_pages + 1):
        pe = jnp.exp2(s_blocks[p] - m)
        l_sum = l_sum + jnp.sum(pe, axis=1, keepdims=True)
        if p < n_pages:
            o = o + _dot_nt(pe.astype(BF16), v_refs[p][...].astype(BF16))
        else:
            o = o + _dot(pe.astype(BF16), vn)
    o = jnp.where(own_head, o / l_sum, 0.0)
    return jnp.concatenate(
        [jnp.sum(o[tk * N_HEADS:(tk + 1) * N_HEADS], axis=0, keepdims=True) for tk in range(t_new)], axis=0)


def _attn_sample_consts(n_pages):
    key = np.arange(PAGE)
    ls = (key[:, None] > key[None, :]).astype(np.float32)
    ones = np.ones((PAGE, LANES), np.float32)
    ph = np.arange(n_pages * N_HEADS)
    m3 = ((ph[:, None] % N_HEADS == ph[None, :] % N_HEADS)
          & (ph[None, :] // N_HEADS > ph[:, None] // N_HEADS)).astype(np.float32)
    return tuple(jnp.asarray(a, BF16) for a in (ls, ones, m3))


def _merge_kernel(transposed, slab, a_ref, ga_ref, bm_ref, sa_ref, sb_ref, x_ref, wpa_ref, wpb_ref, wo_ref,
                  y_ref):
    for sub in range(x_ref.shape[-2] // slab):
        rows = slice(sub * slab, (sub + 1) * slab)
        at = (lambda ref: ref[0, rows]) if transposed else (lambda ref: ref[rows])
        attn = a_ref[0, :, rows].T if transposed else a_ref[rows]
        a = (attn * at(ga_ref).astype(F32)).astype(BF16)
        pa = _dot(a, wpa_ref[...])
        pb = _dot(at(bm_ref), wpb_ref[...])
        m = at(sa_ref).astype(F32) * pa + at(sb_ref).astype(F32) * pb
        y = at(x_ref) + _dot(m.astype(BF16), wo_ref[...])
        if transposed:
            y_ref[0, rows] = y
        else:
            y_ref[rows] = y


def _merge_attn_kernel(layer, n_pages, t_new, seqs, pt_ref, a_ref, ga_ref, bm_ref, sa_ref, sb_ref, x_ref,
                       wpa_ref, wpb_ref, wo_ref, q_ref, kn_ref, vn_ref, ln_ref, ls_ref, ones_ref, m3_ref,
                       ck_hbm, cv_hbm, cl_hbm, y_ref, o_ref, kbuf, vbuf, lbuf, sems):
    nt = pl.num_programs(1)
    st = pl.program_id(0) * nt + pl.program_id(1)
    last = pl.num_programs(0) * nt - 1
    slot = st & 1
    per_step = seqs * n_pages
    kinds = ((ck_hbm, kbuf), (cv_hbm, vbuf), (cl_hbm, lbuf))

    def page_copy(kind, page, to_slot, idx):
        hbm, buf = kinds[kind]
        return pltpu.make_async_copy(hbm.at[layer, page], buf.at[to_slot, idx], sems.at[to_slot, kind])

    def fetch(step, to_slot):
        for idx in range(per_step):
            page = pt_ref[step * seqs + idx // n_pages, idx % n_pages]
            for kind in range(3):
                page_copy(kind, page, to_slot, idx).start()

    @pl.when(st == 0)
    def _():
        fetch(0, 0)

    for idx in range(per_step):
        for kind in range(3):
            page_copy(kind, 0, slot, idx).wait()

    @pl.when(st < last)
    def _():
        fetch(st + 1, 1 - slot)

    _merge_kernel(True, x_ref.shape[-2], a_ref, ga_ref, bm_ref, sa_ref, sb_ref, x_ref, wpa_ref, wpb_ref,
                  wo_ref, y_ref)
    for u in range(seqs):
        tok = slice(u * t_new, (u + 1) * t_new)
        pages = [[buf.at[slot, u * n_pages + p] for p in range(n_pages)] for _, buf in kinds]
        o_ref[tok] = _attn_sample_one(q_ref[tok], kn_ref[tok], vn_ref[tok], ln_ref[tok],
                                      ls_ref, ones_ref, m3_ref, *pages)


def _merge_prompt_attn_sample(layer, page_table, attn_t, ga, bm, sa, sb, x, wpa, wpb, wo, q, kn, vn, lnp,
                              ck, cv, clt, t_new, seqs):
    b, s, _ = x.shape
    n_seq, n_pages = page_table.shape
    nt = n_seq // (seqs * b)
    tm = s // nt
    assert b * nt * seqs == n_seq and nt * tm == s
    consts = _attn_sample_consts(n_pages)
    step = lambda bi, i: bi * nt + i
    row3 = lambda width: pl.BlockSpec((1, tm, width), lambda bi, i, pt: (bi, i, 0))
    tok = lambda width: pl.BlockSpec((seqs * t_new, width), lambda bi, i, pt: (step(bi, i), 0))

    per_step = seqs * n_pages
    grid_spec = pltpu.PrefetchScalarGridSpec(
        num_scalar_prefetch=1,
        grid=(b, nt),
        in_specs=[pl.BlockSpec((1, WIDTH_A, tm), lambda bi, i, pt: (bi, 0, i)),
                  row3(WIDTH_A), row3(WIDTH_B), row3(D_MODEL), row3(D_MODEL), row3(D_MODEL),
                  _const_spec(wpa), _const_spec(wpb), _const_spec(wo),
                  tok(WIDTH_A), tok(WIDTH_A), tok(WIDTH_A), tok(LANES)]
        + [_const_spec(c) for c in consts]
        + [pl.BlockSpec(memory_space=pl.ANY)] * 3,
        out_specs=(row3(D_MODEL), tok(WIDTH_A)),
        scratch_shapes=[pltpu.VMEM((2, per_step, WIDTH_A, PAGE), F32),
                        pltpu.VMEM((2, per_step, WIDTH_A, PAGE), F32),
                        pltpu.VMEM((2, per_step, N_HEADS, PAGE), F32),
                        pltpu.SemaphoreType.DMA((2, 3))],
    )
    return pl.pallas_call(
        functools.partial(_merge_attn_kernel, layer, n_pages, t_new, seqs),
        out_shape=(jax.ShapeDtypeStruct((b, s, D_MODEL), F32),
                   jax.ShapeDtypeStruct((n_seq * t_new, WIDTH_A), F32)),
        grid_spec=grid_spec,
        compiler_params=pltpu.CompilerParams(
            dimension_semantics=("arbitrary", "arbitrary"), vmem_limit_bytes=VMEM_LIMIT),
        name="merge_prompt_attn_sample",
    )(page_table, attn_t, ga, bm, sa, sb, x, wpa, wpb, wo, q, kn, vn, lnp, *consts, ck, cv, clt)


def _merge_sample(attn, ga, bm, sa, sb, x, wpa, wpb, wo, tm):
    n = x.shape[0]
    row2 = lambda width: pl.BlockSpec((tm, width), lambda i: (i, 0))
    return pl.pallas_call(
        functools.partial(_merge_kernel, False, tm),
        out_shape=jax.ShapeDtypeStruct((n, D_MODEL), F32),
        grid=(n // tm,),
        in_specs=[row2(WIDTH_A), row2(WIDTH_A), row2(WIDTH_B), row2(D_MODEL), row2(D_MODEL), row2(D_MODEL),
                  _const_spec(wpa), _const_spec(wpb), _const_spec(wo)],
        out_specs=row2(D_MODEL),
        compiler_params=pltpu.CompilerParams(
            dimension_semantics=("arbitrary",), vmem_limit_bytes=VMEM_LIMIT),
        name="merge_sample",
    )(attn, ga, bm, sa, sb, x, wpa, wpb, wo)


KEY_BLOCK = 256
TM_PROMPT = 512
HEADS_PER_STEP = 8
SEQS_PER_STEP = 2
TM_SAMPLE = 256


def kernel(x_prompt, x_sample, cache_k, cache_v, cache_logf, page_table, norm_g, w_in, b_forget, q_norm_g,
           k_norm_g, v_norm_g, w_s, b_s, w_proj_a, w_proj_b, w_out):
    depth = w_in.shape[0]
    bsz, seq, _ = x_prompt.shape
    n_seq, t_new, _ = x_sample.shape
    n_pool = cache_k.shape[1]

    f_lo = 4 * WIDTH_A
    w_head = w_in[:, :, :f_lo].astype(BF16)
    w_tail = w_in[:, :, f_lo + N_HEADS:].astype(BF16)
    w_forget = jnp.pad(w_in[:, :, f_lo:f_lo + N_HEADS],
                       ((0, 0), (0, 0), (0, C_END - C_F - N_HEADS))).astype(BF16)
    bf_pad = jnp.pad(b_forget, ((0, 0), (0, LANES - N_HEADS)))[:, None, :]
    qg = jnp.tile(q_norm_g, (1, N_HEADS))[:, None, :]
    kg = jnp.tile(k_norm_g, (1, N_HEADS))[:, None, :]
    gain_cols = lambda gn: jnp.tile(gn[:, :, None], (1, 2, KEY_BLOCK))
    qg_t = gain_cols(q_norm_g * (HEAD_DIM ** -0.5 * LOG2E))
    kg_t = gain_cols(k_norm_g)
    vg = v_norm_g[:, None, :]
    ng = norm_g[:, None, :]
    wcat_p = w_s.transpose(0, 2, 1, 3).reshape(depth, CHUNK, N_GROUPS * CHUNK).astype(BF16)
    w_small = jnp.tile(w_s[:, :, :t_new, :t_new], (1, 1, CHUNK // t_new, CHUNK // t_new))
    wcat_s = w_small.transpose(0, 2, 1, 3).reshape(depth, CHUNK, N_GROUPS * CHUNK).astype(BF16)
    sbias_p = jnp.repeat(b_s.transpose(0, 2, 1), GROUP_DIM, axis=-1)
    sbias_s = jnp.repeat(jnp.tile(b_s[:, :, :t_new], (1, 1, CHUNK // t_new)).transpose(0, 2, 1),
                         GROUP_DIM, axis=-1)
    wpa = w_proj_a.astype(BF16)
    wpb = w_proj_b.astype(BF16)
    wo = w_out.astype(BF16)
    ck = cache_k.transpose(0, 1, 3, 4, 2).reshape(depth, n_pool, WIDTH_A, PAGE)
    cv = cache_v.transpose(0, 1, 3, 4, 2).reshape(depth, n_pool, WIDTH_A, PAGE)
    clt = cache_logf.transpose(0, 1, 3, 2)

    xp = x_prompt
    xs = x_sample.reshape(n_seq * t_new, D_MODEL)
    outs = [[] for _ in range(4)]
    stacked = None
    for l in range(depth):
        weights = (w_head[l], w_tail[l], w_forget[l])
        qt, ka, vt, *stacked, ga, bm, sa, sb = _in_proj_prompt(
            xp, (ng[l], weights, bf_pad[l], qg_t[l], kg_t[l], vg[l], wcat_p[l], sbias_p[l]),
            TM_PROMPT, KEY_BLOCK, l, stacked)
        common = (ng[l], weights, bf_pad[l], qg[l], kg[l], vg[l])
        attn_t = _attn_prompt(qt, ka, vt, HEADS_PER_STEP)

        q_s, k_s, v_s, lf_s, lfp_s, vbn_s, ga_s, bm_s, sa_s, sb_s = _in_proj_sample(
            xs, common + (wcat_s[l], sbias_s[l]), TM_SAMPLE, t_new)
        xp, attn_s = _merge_prompt_attn_sample(
            l, page_table, attn_t, ga, bm, sa, sb, xp, wpa[l], wpb[l], wo[l], q_s, k_s, v_s, lfp_s,
            ck, cv, clt, t_new, SEQS_PER_STEP)
        xs = _merge_sample(attn_s, ga_s, bm_s, sa_s, sb_s, xs, wpa[l], wpb[l], wo[l], TM_SAMPLE)
        outs[0].append(k_s.reshape(n_seq, t_new, N_HEADS, HEAD_DIM))
        outs[1].append(v_s.reshape(n_seq, t_new, N_HEADS, HEAD_DIM))
        outs[2].append(lf_s.reshape(n_seq, t_new, N_HEADS))
        outs[3].append(vbn_s.reshape(n_seq, t_new, WIDTH_B))
    kt_all, vt_all, lft_all = stacked
    unstack = lambda a: a.reshape(depth, bsz, N_HEADS, HEAD_DIM, seq).transpose(0, 1, 4, 2, 3)
    return ((xp, xs.reshape(n_seq, t_new, D_MODEL), unstack(kt_all), unstack(vt_all),
             lft_all.transpose(0, 1, 3, 2)) + tuple(jnp.stack(o) for o in outs))
```

```python
import functools
import math

import numpy as np
import jax
import jax.numpy as jnp
from jax import lax
from jax.experimental import pallas as pl
from jax.experimental.pallas import tpu as pltpu

D_MODEL = 1024
N_HEADS = 8
HEAD_DIM = 64
HEAD_BITS = 3
HEAD_DIM_BITS = 6
WIDTH_A = N_HEADS * HEAD_DIM
N_GROUPS = 8
GROUP_DIM = 64
GROUP_DIM_BITS = 6
WIDTH_B = N_GROUPS * GROUP_DIM
CHUNK = 128
PAGE = 128
EPS = 1e-6
LOG2E = math.log2(math.e)
NEG_BIG = -1e30

LANES = 128
VT_ROWS = 80
VMEM_LIMIT = 56 * 1024 * 1024

C_Q, C_K, C_V, C_GA, C_U, C_VB, C_GB, C_SA, C_SB, C_F, C_END = (
    0, 512, 1024, 1536, 2048, 2560, 3072, 3584, 4608, 5632, 5760)

F32 = jnp.float32
BF16 = jnp.bfloat16


def _split3(x):
    hi = x.astype(BF16)
    r1 = x - hi.astype(F32)
    mid = r1.astype(BF16)
    lo = (r1 - mid.astype(F32)).astype(BF16)
    return hi, mid, lo


def _dot(a, b):
    return jnp.dot(a, b, preferred_element_type=F32)


def _dot_nt(a, b):
    return lax.dot_general(a, b, (((1,), (1,)), ((), ())), preferred_element_type=F32)


def _dot3_right(parts, w):
    return _dot(parts[0], w) + _dot(parts[1], w) + _dot(parts[2], w)


def _dot3_left(w, parts):
    return _dot(w, parts[0]) + _dot(w, parts[1]) + _dot(w, parts[2])


def _sigmoid(x):
    return 1.0 / (1.0 + jnp.exp(-x))


def _gelu_tanh(x):
    c = math.sqrt(2.0 / math.pi)
    return 0.5 * x * (1.0 + jnp.tanh(c * (x + 0.044715 * (x * x * x))))


def _log_sigmoid(x):
    return jnp.minimum(x, 0.0) - jnp.log1p(jnp.exp(-jnp.abs(x)))


def _in_proj_common(x, g_ref, w_ref, bf_ref, qg_ref, kg_ref, vg_ref, bd_ref, wcat_ref, sbias_ref,
                    seq_local):
    tm = x.shape[0]
    ms = jnp.mean(x * x, axis=-1, keepdims=True)
    h = ((x * lax.rsqrt(ms + EPS)) * g_ref[...]).astype(BF16)

    def seg(lo, hi):
        w_head, w_tail, w_forget = w_ref
        if hi <= C_U:
            return _dot(h, w_head[:, lo:hi])
        if lo >= C_F:
            return _dot(h, w_forget[...])
        return _dot(h, w_tail[:, lo - C_U:hi - C_U])

    def head_norm(z, gain):
        msz = _dot((z * z).astype(BF16), bd_ref[...]) * (1.0 / HEAD_DIM)
        return z * lax.rsqrt(msz + EPS) * gain

    if bd_ref is None:
        qn, kn = seg(C_Q, C_K), seg(C_K, C_V)
    else:
        qn = head_norm(seg(C_Q, C_K), qg_ref[...])
        kn = head_norm(seg(C_K, C_V), kg_ref[...])
    v = seg(C_V, C_GA)
    zga = seg(C_GA, C_U)
    ga = zga * _sigmoid(zga)

    lane = lax.broadcasted_iota(jnp.int32, (tm, LANES), 1)
    logf = jnp.where(lane < N_HEADS, _log_sigmoid(seg(C_F, C_END) + bf_ref[...]), 0.0)

    u = _gelu_tanh(seg(C_U, C_VB))
    vbg = _gelu_tanh(seg(C_VB, C_GB))
    msv = jnp.mean(vbg * vbg, axis=-1, keepdims=True)
    vbn = vbg * lax.rsqrt(msv + EPS) * vg_ref[...]
    zgb = seg(C_GB, C_SA)
    ugb = u * (zgb * _sigmoid(zgb))

    t_idx = lax.broadcasted_iota(jnp.int32, (CHUNK, N_GROUPS * CHUNK), 0)
    s_idx = lax.broadcasted_iota(jnp.int32, (CHUNK, N_GROUPS * CHUNK), 1) & (CHUNK - 1)
    keep = s_idx <= t_idx
    if seq_local < CHUNK:
        shift = int(math.log2(seq_local))
        keep = keep & ((s_idx >> shift) == (t_idx >> shift))
    wm = jnp.where(keep, wcat_ref[...], jnp.zeros((), BF16))
    gq = N_GROUPS // 2
    glane = lax.broadcasted_iota(jnp.int32, (CHUNK, gq * GROUP_DIM), 1) >> GROUP_DIM_BITS
    vbn_b = vbn.astype(BF16)
    s_chunks = []
    for c in range(tm // CHUNK):
        halves = []
        for hf in range(2):
            vc = vbn_b[c * CHUNK:(c + 1) * CHUNK, hf * gq * GROUP_DIM:(hf + 1) * gq * GROUP_DIM]
            rhs = jnp.concatenate(
                [jnp.where(glane == g, vc, jnp.zeros((), BF16)) for g in range(gq)], axis=0)
            halves.append(_dot(wm[:, hf * gq * CHUNK:(hf + 1) * gq * CHUNK], rhs))
        s_chunks.append(jnp.concatenate(halves, axis=1) + sbias_ref[...])
    s_sgu = jnp.concatenate(s_chunks, axis=0)
    bmix = ugb * s_sgu

    sa = _sigmoid(seg(C_SA, C_SB))
    sb = _sigmoid(seg(C_SB, C_F))
    return qn, kn, v, logf, vbn, ga, bmix, sa, sb


def _in_proj_prompt_kernel(layer, *refs):
    n_prev = 3 if layer else 0
    (x_ref, g_ref, w_ref, bf_ref, qgt_ref, kgt_ref, vg_ref, wcat_ref, sbias_ref, tri_ref, pk_ref,
     pq_ref, qt_ref, ka_ref, vt_ref, ko_ref, vo_ref, lf_ref, ga_ref, bm_ref, sa_ref, sb_ref,
     carry_ref) = refs[n_prev:]
    for prev_ref, out_ref in zip(refs[:n_prev], (ko_ref, vo_ref, lf_ref)):
        out_ref[0:layer] = prev_ref[...]
    sr = vt_ref.shape[-1]

    @pl.when(pl.program_id(1) == 0)
    def _():
        carry_ref[...] = jnp.zeros_like(carry_ref)

    def head_norm_t(zt, gain_t):
        sq = zt * zt
        parts = []
        for hh in range(2):
            ms = jnp.sum(sq[hh * HEAD_DIM:(hh + 1) * HEAD_DIM], axis=0, keepdims=True) * (1.0 / HEAD_DIM)
            parts.append(jnp.broadcast_to(lax.rsqrt(ms + EPS), (HEAD_DIM, sr)))
        return zt * jnp.concatenate(parts, axis=0) * gain_t

    carry = carry_ref[...]
    lane = lax.broadcasted_iota(jnp.int32, (sr, LANES), 1)
    row8 = lax.broadcasted_iota(jnp.int32, (N_HEADS, sr), 0)
    pad_rows = jnp.zeros((LANES - N_HEADS, sr), F32)
    ones_blk = jnp.where(lax.broadcasted_iota(jnp.int32, (VT_ROWS - HEAD_DIM, sr), 0) == 0, 1.0, 0.0)
    for sub in range(x_ref.shape[-2] // sr):
        rows = slice(sub * sr, (sub + 1) * sr)
        zq, zk, v, logf, _, ga, bmix, sa, sb = _in_proj_common(
            x_ref[0, rows], g_ref, w_ref, bf_ref, None, None, vg_ref, None, wcat_ref, sbias_ref, CHUNK)

        lft = logf.T[0:N_HEADS]
        lf_ref[layer, 0, :, rows] = lft
        ga_ref[0, rows] = ga.astype(BF16)
        bm_ref[0, rows] = bmix.astype(BF16)
        sa_ref[0, rows] = sa.astype(BF16)
        sb_ref[0, rows] = sb.astype(BF16)

        stacked = jnp.concatenate([part.astype(F32) for part in _split3(lft)]
                                  + [jnp.zeros((N_HEADS, sr), F32)], axis=0).astype(BF16)
        cs = _dot(stacked, tri_ref[...])
        ct = cs[0:N_HEADS] + cs[N_HEADS:2 * N_HEADS] + cs[2 * N_HEADS:3 * N_HEADS] + carry
        carry = jnp.broadcast_to(ct[:, sr - 1:sr], (N_HEADS, sr))
        c2t = ct * LOG2E
        c2 = jnp.concatenate([c2t, pad_rows], axis=0).T

        hi, mid, lo = _split3(c2)
        cp = (hi.astype(F32) + pltpu.roll(mid.astype(F32), 8, axis=1)
              + pltpu.roll(lo.astype(F32), 16, axis=1) + jnp.where(lane == 24, 1.0, 0.0))
        extra_k = _dot(cp.astype(BF16), pk_ref[...])

        hit, midt, lot = (part.astype(F32) for part in _split3(c2t))
        cpt = jnp.concatenate(
            [hit, midt, lot, jnp.where(row8 == 0, 1.0, 0.0), jnp.zeros((LANES - 32, sr), F32)],
            axis=0).astype(BF16)
        extra_q = _dot(pq_ref[...], cpt)

        for j in range(N_HEADS // 2):
            qs_t = head_norm_t(zq[:, j * LANES:(j + 1) * LANES].T, qgt_ref[...])
            ks_t = head_norm_t(zk[:, j * LANES:(j + 1) * LANES].T, kgt_ref[...])
            vs_t = v[:, j * LANES:(j + 1) * LANES].T
            ks = ks_t.T
            ks_sw = pltpu.roll(ks, HEAD_DIM, axis=1)
            ko_ref[layer, 0, j * LANES:(j + 1) * LANES, rows] = ks_t
            vo_ref[layer, 0, j * LANES:(j + 1) * LANES, rows] = vs_t
            for hh in range(2):
                hd = 2 * j + hh
                qt_ref[0, hd, :, rows] = jnp.concatenate(
                    [qs_t[hh * HEAD_DIM:(hh + 1) * HEAD_DIM], extra_q[hd * HEAD_DIM:(hd + 1) * HEAD_DIM]],
                    axis=0).astype(BF16)
                vt_ref[0, hd, sub] = jnp.concatenate(
                    [vs_t[hh * HEAD_DIM:(hh + 1) * HEAD_DIM], ones_blk], axis=0).astype(BF16)
                ka_ref[0, hd, rows] = jnp.where(lane < HEAD_DIM, ks if hh == 0 else ks_sw,
                                                extra_k).astype(BF16)
    carry_ref[...] = carry


def _in_proj_sample_kernel(x_ref, g_ref, w_ref, bf_ref, qg_ref, kg_ref, vg_ref, bd_ref, wcat_ref,
                           sbias_ref,
                           q_ref, ko_ref, vo_ref, lf_ref, lfp_ref, vbn_ref, ga_ref, bm_ref, sa_ref, sb_ref,
                           *, seq_local):
    qn, kn, v, logf, vbn, ga, bmix, sa, sb = _in_proj_common(
        x_ref[...], g_ref, w_ref, bf_ref, qg_ref, kg_ref, vg_ref, bd_ref, wcat_ref, sbias_ref, seq_local)
    q_ref[...] = qn * (HEAD_DIM ** -0.5 * LOG2E)
    ko_ref[...] = kn
    vo_ref[...] = v
    lf_ref[...] = logf[:, :N_HEADS]
    lfp_ref[...] = logf
    vbn_ref[...] = vbn
    ga_ref[...] = ga.astype(BF16)
    bm_ref[...] = bmix.astype(BF16)
    sa_ref[...] = sa.astype(BF16)
    sb_ref[...] = sb.astype(BF16)


def _const_spec(arr):
    if isinstance(arr, tuple):
        return tuple(_const_spec(a) for a in arr)
    return pl.BlockSpec(arr.shape, lambda *_: (0,) * arr.ndim, pipeline_mode=pl.Buffered(1))


def _in_proj_consts(tm):
    bd = np.kron(np.eye(N_HEADS, dtype=np.float32), np.ones((HEAD_DIM, HEAD_DIM), np.float32))
    tri = np.triu(np.ones((tm, tm), np.float32))
    pk = np.zeros((LANES, LANES), np.float32)
    pq = np.zeros((N_HEADS * HEAD_DIM, LANES), np.float32)
    for hd in range(N_HEADS):
        for part in range(3):
            pk[8 * part + hd, HEAD_DIM + 3 * hd + part] = -1.0
            pk[24, HEAD_DIM + 3 * N_HEADS + part] = 1.0
            pq[hd * HEAD_DIM + 3 * hd + part, 24] = 1.0
            pq[hd * HEAD_DIM + 3 * N_HEADS + part, 8 * part + hd] = 1.0
    return (jnp.asarray(bd, BF16), jnp.asarray(tri, BF16), jnp.asarray(pk, BF16), jnp.asarray(pq, BF16))


def _in_proj_prompt(x, params, tm, slab, layer, stacked):
    b, s, _ = x.shape
    nt = s // tm
    _, tri, pk, pq = _in_proj_consts(slab)
    g, w, bf, qgt, kgt, vg, wcat, sbias = params
    consts = (g, w, bf, qgt, kgt, vg, wcat, sbias, tri, pk, pq)
    stacked = () if stacked is None else tuple(stacked)
    assert len(stacked) == (3 if layer else 0)
    row3 = lambda width: pl.BlockSpec((1, tm, width), lambda bi, i: (bi, i, 0))
    col4 = lambda n, rows: pl.BlockSpec((n, 1, rows, tm), lambda bi, i: (0, bi, 0, i))
    depth = layer + 1
    out_shape = (
        jax.ShapeDtypeStruct((b, N_HEADS, LANES, s), BF16),
        jax.ShapeDtypeStruct((b, N_HEADS, s, LANES), BF16),
        jax.ShapeDtypeStruct((b, N_HEADS, s // slab, VT_ROWS, slab), BF16),
        jax.ShapeDtypeStruct((depth, b, WIDTH_A, s), F32),
        jax.ShapeDtypeStruct((depth, b, WIDTH_A, s), F32),
        jax.ShapeDtypeStruct((depth, b, N_HEADS, s), F32),
        jax.ShapeDtypeStruct((b, s, WIDTH_A), BF16),
        jax.ShapeDtypeStruct((b, s, WIDTH_B), BF16),
        jax.ShapeDtypeStruct((b, s, D_MODEL), BF16),
        jax.ShapeDtypeStruct((b, s, D_MODEL), BF16),
    )
    out_specs = (
        pl.BlockSpec((1, N_HEADS, LANES, tm), lambda bi, i: (bi, 0, 0, i)),
        pl.BlockSpec((1, N_HEADS, tm, LANES), lambda bi, i: (bi, 0, i, 0)),
        pl.BlockSpec((1, N_HEADS, tm // slab, VT_ROWS, slab), lambda bi, i: (bi, 0, i, 0, 0)),
        col4(depth, WIDTH_A), col4(depth, WIDTH_A), col4(depth, N_HEADS), row3(WIDTH_A), row3(WIDTH_B),
        row3(D_MODEL), row3(D_MODEL),
    )
    prev_specs = [col4(layer, WIDTH_A), col4(layer, WIDTH_A), col4(layer, N_HEADS)] if layer else []
    return pl.pallas_call(
        functools.partial(_in_proj_prompt_kernel, layer),
        out_shape=out_shape,
        grid=(b, nt),
        in_specs=prev_specs + [row3(D_MODEL)] + [_const_spec(c) for c in consts],
        out_specs=out_specs,
        scratch_shapes=[pltpu.VMEM((N_HEADS, slab), F32)],
        compiler_params=pltpu.CompilerParams(
            dimension_semantics=("arbitrary", "arbitrary"), vmem_limit_bytes=VMEM_LIMIT),
        name="in_proj_prompt",
    )(*stacked, x, *consts)


def _in_proj_sample(x, params, tm, seq_local):
    n = x.shape[0]
    bd = _in_proj_consts(tm)[0]
    g, w, bf, qg, kg, vg, wcat, sbias = params
    consts = (g, w, bf, qg, kg, vg, bd, wcat, sbias)
    row2 = lambda width: pl.BlockSpec((tm, width), lambda i: (i, 0))
    out_shape = (
        jax.ShapeDtypeStruct((n, WIDTH_A), F32),
        jax.ShapeDtypeStruct((n, WIDTH_A), F32),
        jax.ShapeDtypeStruct((n, WIDTH_A), F32),
        jax.ShapeDtypeStruct((n, N_HEADS), F32),
        jax.ShapeDtypeStruct((n, LANES), F32),
        jax.ShapeDtypeStruct((n, WIDTH_B), F32),
        jax.ShapeDtypeStruct((n, WIDTH_A), BF16),
        jax.ShapeDtypeStruct((n, WIDTH_B), BF16),
        jax.ShapeDtypeStruct((n, D_MODEL), BF16),
        jax.ShapeDtypeStruct((n, D_MODEL), BF16),
    )
    out_specs = (row2(WIDTH_A), row2(WIDTH_A), row2(WIDTH_A), row2(N_HEADS), row2(LANES), row2(WIDTH_B),
                 row2(WIDTH_A), row2(WIDTH_B), row2(D_MODEL), row2(D_MODEL))
    return pl.pallas_call(
        functools.partial(_in_proj_sample_kernel, seq_local=seq_local),
        out_shape=out_shape,
        grid=(n // tm,),
        in_specs=[row2(D_MODEL)] + [_const_spec(c) for c in consts],
        out_specs=out_specs,
        compiler_params=pltpu.CompilerParams(
            dimension_semantics=("arbitrary",), vmem_limit_bytes=VMEM_LIMIT),
        name="in_proj_sample",
    )(x, *consts)


def _attn_prompt_kernel(qt_ref, ka_ref, vt_ref, o_ref, s_ref, m_ref, al_ref, acc_ref):
    g = qt_ref.shape[1]
    tq = qt_ref.shape[-1]
    t = vt_ref.shape[-1]
    assert tq == 2 * t
    i = pl.program_id(2)
    key = lax.broadcasted_iota(jnp.int32, (t, tq), 0)
    qry = lax.broadcasted_iota(jnp.int32, (t, tq), 1)

    def scores(hd, blk, slot, mask, m_old):
        k = ka_ref[0, hd, pl.ds(pl.multiple_of(blk * t, t), t), :]
        s = _dot(k, qt_ref[0, hd])
        if mask is not None:
            s = jnp.where(mask, s, NEG_BIG)
        s_ref[slot, hd] = s
        m_new = jnp.maximum(m_old, jnp.max(s, axis=0, keepdims=True))
        m_ref[slot, hd, 0:1, :] = m_new
        al_ref[slot, hd, 0:1, :] = jnp.exp2(m_old - m_new)

    def iteration(j, cur, next_mask, has_next):
        for hd in range(g):
            p = jnp.exp2(s_ref[cur, hd] - m_ref[cur, hd, 0:1, :]).astype(BF16)
            if has_next:
                scores(hd, j + 1, 1 - cur, next_mask, m_ref[cur, hd, 0:1, :])
            acc_ref[hd] = al_ref[cur, hd, 0:1, :] * acc_ref[hd] + _dot(vt_ref[0, hd, j], p)

    acc_ref[...] = jnp.zeros_like(acc_ref)
    diag_lo = key <= qry
    diag_hi = key + t <= qry
    for hd in range(g):
        scores(hd, 0, 0, key <= qry + jnp.where(i > 0, tq, 0), jnp.full((1, tq), NEG_BIG, F32))

    def pair(jj, carry):
        iteration(2 * jj, 0, None, True)
        iteration(2 * jj + 1, 1, None, True)
        return carry

    lax.fori_loop(0, i - 1, pair, 0)

    @pl.when(i >= 1)
    def _():
        iteration(2 * i - 2, 0, None, True)
        iteration(2 * i - 1, 1, diag_lo, True)

    last = 2 * i + 1
    iteration(last - 1, 0, diag_hi, True)
    iteration(last, 1, None, False)
    for hd in range(g):
        acc = acc_ref[hd]
        o_ref[0, hd * HEAD_DIM:(hd + 1) * HEAD_DIM, :] = acc[0:HEAD_DIM] / acc[HEAD_DIM:HEAD_DIM + 1]


def _attn_prompt(qt, ka, vt, g):
    b, h, _, s = qt.shape
    t = vt.shape[-1]
    tq = 2 * t
    return pl.pallas_call(
        _attn_prompt_kernel,
        out_shape=jax.ShapeDtypeStruct((b, WIDTH_A, s), F32),
        grid=(b, h // g, s // tq),
        in_specs=[
            pl.BlockSpec((1, g, LANES, tq), lambda bi, hi, i: (bi, hi, 0, i)),
            pl.BlockSpec((1, g, s, LANES), lambda bi, hi, i: (bi, hi, 0, 0),
                         pipeline_mode=pl.Buffered(1)),
            pl.BlockSpec((1, g, s // t, VT_ROWS, t), lambda bi, hi, i: (bi, hi, 0, 0, 0),
                         pipeline_mode=pl.Buffered(1)),
        ],
        out_specs=pl.BlockSpec((1, g * HEAD_DIM, tq), lambda bi, hi, i: (bi, hi, i)),
        scratch_shapes=[pltpu.VMEM((2, g, t, tq), F32),
                        pltpu.VMEM((2, g, 8, tq), F32), pltpu.VMEM((2, g, 8, tq), F32),
                        pltpu.VMEM((g, VT_ROWS, tq), F32)],
        compiler_params=pltpu.CompilerParams(
            dimension_semantics=("arbitrary", "arbitrary", "arbitrary"), vmem_limit_bytes=VMEM_LIMIT),
        name="attn_prompt",
    )(qt, ka, vt)


def _attn_sample_one(q, k_new, v_new, logf_new, ls_ref, ones_ref, m3_ref, k_refs, v_refs, l_refs):
    t_new = q.shape[0]
    n_pages = len(k_refs)
    rows = t_new * N_HEADS

    def per_token(x):
        return jnp.concatenate(
            [jnp.broadcast_to(x[tk:tk + 1], (N_HEADS, x.shape[1])) for tk in range(t_new)], axis=0)

    row = lax.broadcasted_iota(jnp.int32, (rows, WIDTH_A), 0)
    lane = lax.broadcasted_iota(jnp.int32, (rows, WIDTH_A), 1)
    own_head = (row & (N_HEADS - 1)) == (lane >> HEAD_DIM_BITS)
    qbd = jnp.where(own_head, per_token(q), 0.0).astype(BF16)

    cn = logf_new * LOG2E
    trow = lax.broadcasted_iota(jnp.int32, (t_new, LANES), 0)
    sh = 1
    while sh < t_new:
        cn = cn + jnp.where(trow >= sh, pltpu.roll(cn, sh, axis=0), 0.0)
        sh *= 2
    row1 = lax.broadcasted_iota(jnp.int32, (rows, LANES), 0)
    lane1 = lax.broadcasted_iota(jnp.int32, (rows, LANES), 1)
    head_lane = (row1 & (N_HEADS - 1)) == lane1
    cn_col = jnp.sum(jnp.where(head_lane, per_token(cn), 0.0), axis=1, keepdims=True)

    x = jnp.concatenate([l_refs[p][...] for p in range(n_pages)], axis=0) * LOG2E
    xp = _split3(x)
    r_in = _dot3_right(xp, ls_ref[...])
    tot_b = _dot3_right(xp, ones_ref[...])
    r_page = _dot3_left(m3_ref[...], _split3(tot_b))
    r = r_in + r_page

    s_blocks = []
    for p in range(n_pages):
        kp_t = k_refs[p][...].astype(BF16)
        r_p = r[p * N_HEADS:(p + 1) * N_HEADS]
        s_blocks.append(_dot(qbd, kp_t) + jnp.concatenate([r_p] * t_new, axis=0) + cn_col)
    pad = jnp.zeros((PAGE - t_new, WIDTH_A), F32)
    kn = jnp.concatenate([k_new, pad], axis=0).astype(BF16)
    vn = jnp.concatenate([v_new, pad], axis=0).astype(BF16)
    cn_pad = jnp.concatenate([cn, jnp.zeros((PAGE - t_new, LANES), F32)], axis=0)
    sel = jnp.where(head_lane, 1.0, 0.0).astype(BF16)
    cn_keys = sum(_dot_nt(sel, part) for part in _split3(cn_pad))
    s_new = _dot_nt(qbd, kn) + cn_col - cn_keys
    s_new = jnp.where((lane1 <= (row1 >> HEAD_BITS)) & (lane1 < t_new), s_new, NEG_BIG)
    s_blocks.append(s_new)

    m = s_blocks[0]
    for sb in s_blocks[1:]:
        m = jnp.maximum(m, sb)
    m = jnp.max(m, axis=1, keepdims=True)
    l_sum = jnp.zeros((rows, 1), F32)
    o = jnp.zeros((rows, WIDTH_A), F32)
    for p in range(n_pages + 1):
        pe = jnp.exp2(s_blocks[p] - m)
        l_sum = l_sum + jnp.sum(pe, axis=1, keepdims=True)
        if p < n_pages:
            o = o + _dot_nt(pe.astype(BF16), v_refs[p][...].astype(BF16))
        else:
            o = o + _dot(pe.astype(BF16), vn)
    o = jnp.where(own_head, o / l_sum, 0.0)
    return jnp.concatenate(
        [jnp.sum(o[tk * N_HEADS:(tk + 1) * N_HEADS], axis=0, keepdims=True) for tk in range(t_new)], axis=0)


def _attn_sample_consts(n_pages):
    key = np.arange(PAGE)
    ls = (key[:, None] > key[None, :]).astype(np.float32)
    ones = np.ones((PAGE, LANES), np.float32)
    ph = np.arange(n_pages * N_HEADS)
    m3 = ((ph[:, None] % N_HEADS == ph[None, :] % N_HEADS)
          & (ph[None, :] // N_HEADS > ph[:, None] // N_HEADS)).astype(np.float32)
    return tuple(jnp.asarray(a, BF16) for a in (ls, ones, m3))


def _merge_kernel(transposed, slab, a_ref, ga_ref, bm_ref, sa_ref, sb_ref, x_ref, wpa_ref, wpb_ref, wo_ref,
                  y_ref):
    for sub in range(x_ref.shape[-2] // slab):
        rows = slice(sub * slab, (sub + 1) * slab)
        at = (lambda ref: ref[0, rows]) if transposed else (lambda ref: ref[rows])
        attn = a_ref[0, :, rows].T if transposed else a_ref[rows]
        a = (attn * at(ga_ref).astype(F32)).astype(BF16)
        pa = _dot(a, wpa_ref[...])
        pb = _dot(at(bm_ref), wpb_ref[...])
        m = at(sa_ref).astype(F32) * pa + at(sb_ref).astype(F32) * pb
        y = at(x_ref) + _dot(m.astype(BF16), wo_ref[...])
        if transposed:
            y_ref[0, rows] = y
        else:
            y_ref[rows] = y


def _merge_attn_kernel(layer, n_pages, t_new, seqs, pt_ref, a_ref, ga_ref, bm_ref, sa_ref, sb_ref, x_ref,
                       wpa_ref, wpb_ref, wo_ref, q_ref, kn_ref, vn_ref, ln_ref, ls_ref, ones_ref, m3_ref,
                       ck_hbm, cv_hbm, cl_hbm, y_ref, o_ref, kbuf, vbuf, lbuf, sems):
    nt = pl.num_programs(1)
    st = pl.program_id(0) * nt + pl.program_id(1)
    last = pl.num_programs(0) * nt - 1
    slot = lax.rem(st, PAGE_SLOTS)
    ahead = PAGE_SLOTS - 1
    per_step = seqs * n_pages
    kinds = ((ck_hbm, kbuf), (cv_hbm, vbuf), (cl_hbm, lbuf))

    def page_copy(kind, page, to_slot, idx):
        hbm, buf = kinds[kind]
        return pltpu.make_async_copy(hbm.at[layer, page], buf.at[to_slot, idx], sems.at[to_slot, kind])

    def fetch(step, to_slot):
        for idx in range(per_step):
            page = pt_ref[step * seqs + idx // n_pages, idx % n_pages]
            for kind in range(3):
                page_copy(kind, page, to_slot, idx).start()

    @pl.when(st == 0)
    def _():
        for step in range(ahead):
            fetch(step, step)

    @pl.when(st + ahead <= last)
    def _():
        fetch(st + ahead, lax.rem(st + ahead, PAGE_SLOTS))

    _merge_kernel(True, x_ref.shape[-2], a_ref, ga_ref, bm_ref, sa_ref, sb_ref, x_ref, wpa_ref, wpb_ref,
                  wo_ref, y_ref)

    for idx in range(per_step):
        for kind in range(3):
            page_copy(kind, 0, slot, idx).wait()
    for u in range(seqs):
        tok = slice(u * t_new, (u + 1) * t_new)
        pages = [[buf.at[slot, u * n_pages + p] for p in range(n_pages)] for _, buf in kinds]
        o_ref[tok] = _attn_sample_one(q_ref[tok], kn_ref[tok], vn_ref[tok], ln_ref[tok],
                                      ls_ref, ones_ref, m3_ref, *pages)


def _merge_prompt_attn_sample(layer, page_table, attn_t, ga, bm, sa, sb, x, wpa, wpb, wo, q, kn, vn, lnp,
                              ck, cv, clt, t_new, seqs):
    b, s, _ = x.shape
    n_seq, n_pages = page_table.shape
    nt = n_seq // (seqs * b)
    tm = s // nt
    assert b * nt * seqs == n_seq and nt * tm == s
    consts = _attn_sample_consts(n_pages)
    step = lambda bi, i: bi * nt + i
    row3 = lambda width: pl.BlockSpec((1, tm, width), lambda bi, i, pt: (bi, i, 0))
    tok = lambda width: pl.BlockSpec((seqs * t_new, width), lambda bi, i, pt: (step(bi, i), 0))

    per_step = seqs * n_pages
    grid_spec = pltpu.PrefetchScalarGridSpec(
        num_scalar_prefetch=1,
        grid=(b, nt),
        in_specs=[pl.BlockSpec((1, WIDTH_A, tm), lambda bi, i, pt: (bi, 0, i)),
                  row3(WIDTH_A), row3(WIDTH_B), row3(D_MODEL), row3(D_MODEL), row3(D_MODEL),
                  _const_spec(wpa), _const_spec(wpb), _const_spec(wo),
                  tok(WIDTH_A), tok(WIDTH_A), tok(WIDTH_A), tok(LANES)]
        + [_const_spec(c) for c in consts]
        + [pl.BlockSpec(memory_space=pl.ANY)] * 3,
        out_specs=(row3(D_MODEL), tok(WIDTH_A)),
        scratch_shapes=[pltpu.VMEM((PAGE_SLOTS, per_step, WIDTH_A, PAGE), F32),
                        pltpu.VMEM((PAGE_SLOTS, per_step, WIDTH_A, PAGE), F32),
                        pltpu.VMEM((PAGE_SLOTS, per_step, N_HEADS, PAGE), F32),
                        pltpu.SemaphoreType.DMA((PAGE_SLOTS, 3))],
    )
    return pl.pallas_call(
        functools.partial(_merge_attn_kernel, layer, n_pages, t_new, seqs),
        out_shape=(jax.ShapeDtypeStruct((b, s, D_MODEL), F32),
                   jax.ShapeDtypeStruct((n_seq * t_new, WIDTH_A), F32)),
        grid_spec=grid_spec,
        compiler_params=pltpu.CompilerParams(
            dimension_semantics=("arbitrary", "arbitrary"), vmem_limit_bytes=VMEM_LIMIT),
        name="merge_prompt_attn_sample",
    )(page_table, attn_t, ga, bm, sa, sb, x, wpa, wpb, wo, q, kn, vn, lnp, *consts, ck, cv, clt)


def _merge_sample(attn, ga, bm, sa, sb, x, wpa, wpb, wo, tm):
    n = x.shape[0]
    row2 = lambda width: pl.BlockSpec((tm, width), lambda i: (i, 0))
    return pl.pallas_call(
        functools.partial(_merge_kernel, False, tm),
        out_shape=jax.ShapeDtypeStruct((n, D_MODEL), F32),
        grid=(n // tm,),
        in_specs=[row2(WIDTH_A), row2(WIDTH_A), row2(WIDTH_B), row2(D_MODEL), row2(D_MODEL), row2(D_MODEL),
                  _const_spec(wpa), _const_spec(wpb), _const_spec(wo)],
        out_specs=row2(D_MODEL),
        compiler_params=pltpu.CompilerParams(
            dimension_semantics=("arbitrary",), vmem_limit_bytes=VMEM_LIMIT),
        name="merge_sample",
    )(attn, ga, bm, sa, sb, x, wpa, wpb, wo)


KEY_BLOCK = 256
TM_PROMPT = 512
HEADS_PER_STEP = 8
SEQS_PER_STEP = 1
PAGE_SLOTS = 3
TM_SAMPLE = 256


def kernel(x_prompt, x_sample, cache_k, cache_v, cache_logf, page_table, norm_g, w_in, b_forget, q_norm_g,
           k_norm_g, v_norm_g, w_s, b_s, w_proj_a, w_proj_b, w_out):
    depth = w_in.shape[0]
    bsz, seq, _ = x_prompt.shape
    n_seq, t_new, _ = x_sample.shape
    n_pool = cache_k.shape[1]

    f_lo = 4 * WIDTH_A
    w_head = w_in[:, :, :f_lo].astype(BF16)
    w_tail = w_in[:, :, f_lo + N_HEADS:].astype(BF16)
    w_forget = jnp.pad(w_in[:, :, f_lo:f_lo + N_HEADS],
                       ((0, 0), (0, 0), (0, C_END - C_F - N_HEADS))).astype(BF16)
    bf_pad = jnp.pad(b_forget, ((0, 0), (0, LANES - N_HEADS)))[:, None, :]
    qg = jnp.tile(q_norm_g, (1, N_HEADS))[:, None, :]
    kg = jnp.tile(k_norm_g, (1, N_HEADS))[:, None, :]
    gain_cols = lambda gn: jnp.tile(gn[:, :, None], (1, 2, KEY_BLOCK))
    qg_t = gain_cols(q_norm_g * (HEAD_DIM ** -0.5 * LOG2E))
    kg_t = gain_cols(k_norm_g)
    vg = v_norm_g[:, None, :]
    ng = norm_g[:, None, :]
    wcat_p = w_s.transpose(0, 2, 1, 3).reshape(depth, CHUNK, N_GROUPS * CHUNK).astype(BF16)
    w_small = jnp.tile(w_s[:, :, :t_new, :t_new], (1, 1, CHUNK // t_new, CHUNK // t_new))
    wcat_s = w_small.transpose(0, 2, 1, 3).reshape(depth, CHUNK, N_GROUPS * CHUNK).astype(BF16)
    sbias_p = jnp.repeat(b_s.transpose(0, 2, 1), GROUP_DIM, axis=-1)
    sbias_s = jnp.repeat(jnp.tile(b_s[:, :, :t_new], (1, 1, CHUNK // t_new)).transpose(0, 2, 1),
                         GROUP_DIM, axis=-1)
    wpa = w_proj_a.astype(BF16)
    wpb = w_proj_b.astype(BF16)
    wo = w_out.astype(BF16)
    ck = cache_k.transpose(0, 1, 3, 4, 2).reshape(depth, n_pool, WIDTH_A, PAGE)
    cv = cache_v.transpose(0, 1, 3, 4, 2).reshape(depth, n_pool, WIDTH_A, PAGE)
    clt = cache_logf.transpose(0, 1, 3, 2)

    xp = x_prompt
    xs = x_sample.reshape(n_seq * t_new, D_MODEL)
    outs = [[] for _ in range(4)]
    stacked = None
    for l in range(depth):
        weights = (w_head[l], w_tail[l], w_forget[l])
        qt, ka, vt, *stacked, ga, bm, sa, sb = _in_proj_prompt(
            xp, (ng[l], weights, bf_pad[l], qg_t[l], kg_t[l], vg[l], wcat_p[l], sbias_p[l]),
            TM_PROMPT, KEY_BLOCK, l, stacked)
        common = (ng[l], weights, bf_pad[l], qg[l], kg[l], vg[l])
        attn_t = _attn_prompt(qt, ka, vt, HEADS_PER_STEP)

        q_s, k_s, v_s, lf_s, lfp_s, vbn_s, ga_s, bm_s, sa_s, sb_s = _in_proj_sample(
            xs, common + (wcat_s[l], sbias_s[l]), TM_SAMPLE, t_new)
        xp, attn_s = _merge_prompt_attn_sample(
            l, page_table, attn_t, ga, bm, sa, sb, xp, wpa[l], wpb[l], wo[l], q_s, k_s, v_s, lfp_s,
            ck, cv, clt, t_new, SEQS_PER_STEP)
        xs = _merge_sample(attn_s, ga_s, bm_s, sa_s, sb_s, xs, wpa[l], wpb[l], wo[l], TM_SAMPLE)
        outs[0].append(k_s.reshape(n_seq, t_new, N_HEADS, HEAD_DIM))
        outs[1].append(v_s.reshape(n_seq, t_new, N_HEADS, HEAD_DIM))
        outs[2].append(lf_s.reshape(n_seq, t_new, N_HEADS))
        outs[3].append(vbn_s.reshape(n_seq, t_new, WIDTH_B))
    kt_all, vt_all, lft_all = stacked
    unstack = lambda a: a.reshape(depth, bsz, N_HEADS, HEAD_DIM, seq).transpose(0, 1, 4, 2, 3)
    return ((xp, xs.reshape(n_seq, t_new, D_MODEL), unstack(kt_all), unstack(vt_all),
             lft_all.transpose(0, 1, 3, 2)) + tuple(jnp.stack(o) for o in outs))
```

```python
import functools
import math

import numpy as np
import jax
import jax.numpy as jnp
from jax import lax
from jax.experimental import pallas as pl
from jax.experimental.pallas import tpu as pltpu

D_MODEL = 1024
N_HEADS = 8
HEAD_DIM = 64
HEAD_BITS = 3
HEAD_DIM_BITS = 6
WIDTH_A = N_HEADS * HEAD_DIM
N_GROUPS = 8
GROUP_DIM = 64
GROUP_DIM_BITS = 6
WIDTH_B = N_GROUPS * GROUP_DIM
CHUNK = 128
PAGE = 128
EPS = 1e-6
LOG2E = math.log2(math.e)
NEG_BIG = -1e30

LANES = 128
VT_ROWS = 80
VMEM_LIMIT = 56 * 1024 * 1024

C_Q, C_K, C_V, C_GA, C_U, C_VB, C_GB, C_SA, C_SB, C_F, C_END = (
    0, 512, 1024, 1536, 2048, 2560, 3072, 3584, 4608, 5632, 5760)

F32 = jnp.float32
BF16 = jnp.bfloat16


def _split3(x):
    hi = x.astype(BF16)
    r1 = x - hi.astype(F32)
    mid = r1.astype(BF16)
    lo = (r1 - mid.astype(F32)).astype(BF16)
    return hi, mid, lo


def _dot(a, b):
    return jnp.dot(a, b, preferred_element_type=F32)


def _dot_nt(a, b):
    return lax.dot_general(a, b, (((1,), (1,)), ((), ())), preferred_element_type=F32)


def _dot3_right(parts, w):
    return _dot(parts[0], w) + _dot(parts[1], w) + _dot(parts[2], w)


def _dot3_left(w, parts):
    return _dot(w, parts[0]) + _dot(w, parts[1]) + _dot(w, parts[2])


def _sigmoid(x):
    return 1.0 / (1.0 + jnp.exp(-x))


def _gelu_tanh(x):
    c = math.sqrt(2.0 / math.pi)
    return 0.5 * x * (1.0 + jnp.tanh(c * (x + 0.044715 * (x * x * x))))


def _log_sigmoid(x):
    return jnp.minimum(x, 0.0) - jnp.log1p(jnp.exp(-jnp.abs(x)))


def _in_proj_common(x, g_ref, w_ref, bf_ref, qg_ref, kg_ref, vg_ref, bd_ref, wcat_ref, sbias_ref,
                    seq_local):
    tm = x.shape[0]
    ms = jnp.mean(x * x, axis=-1, keepdims=True)
    h = ((x * lax.rsqrt(ms + EPS)) * g_ref[...]).astype(BF16)

    def seg(lo, hi):
        w_head, w_tail, w_forget = w_ref
        if hi <= C_U:
            return _dot(h, w_head[:, lo:hi])
        if lo >= C_F:
            return _dot(h, w_forget[...])
        return _dot(h, w_tail[:, lo - C_U:hi - C_U])

    def head_norm(z, gain):
        msz = _dot((z * z).astype(BF16), bd_ref[...]) * (1.0 / HEAD_DIM)
        return z * lax.rsqrt(msz + EPS) * gain

    if bd_ref is None:
        qn, kn = seg(C_Q, C_K), seg(C_K, C_V)
    else:
        qn = head_norm(seg(C_Q, C_K), qg_ref[...])
        kn = head_norm(seg(C_K, C_V), kg_ref[...])
    v = seg(C_V, C_GA)
    zga = seg(C_GA, C_U)
    ga = zga * _sigmoid(zga)

    lane = lax.broadcasted_iota(jnp.int32, (tm, LANES), 1)
    logf = jnp.where(lane < N_HEADS, _log_sigmoid(seg(C_F, C_END) + bf_ref[...]), 0.0)

    u = _gelu_tanh(seg(C_U, C_VB))
    vbg = _gelu_tanh(seg(C_VB, C_GB))
    msv = jnp.mean(vbg * vbg, axis=-1, keepdims=True)
    vbn = vbg * lax.rsqrt(msv + EPS) * vg_ref[...]
    zgb = seg(C_GB, C_SA)
    ugb = u * (zgb * _sigmoid(zgb))

    t_idx = lax.broadcasted_iota(jnp.int32, (CHUNK, N_GROUPS * CHUNK), 0)
    s_idx = lax.broadcasted_iota(jnp.int32, (CHUNK, N_GROUPS * CHUNK), 1) & (CHUNK - 1)
    keep = s_idx <= t_idx
    if seq_local < CHUNK:
        shift = int(math.log2(seq_local))
        keep = keep & ((s_idx >> shift) == (t_idx >> shift))
    wm = jnp.where(keep, wcat_ref[...], jnp.zeros((), BF16))
    gq = N_GROUPS // 2
    glane = lax.broadcasted_iota(jnp.int32, (CHUNK, gq * GROUP_DIM), 1) >> GROUP_DIM_BITS
    vbn_b = vbn.astype(BF16)
    s_chunks = []
    for c in range(tm // CHUNK):
        halves = []
        for hf in range(2):
            vc = vbn_b[c * CHUNK:(c + 1) * CHUNK, hf * gq * GROUP_DIM:(hf + 1) * gq * GROUP_DIM]
            rhs = jnp.concatenate(
                [jnp.where(glane == g, vc, jnp.zeros((), BF16)) for g in range(gq)], axis=0)
            halves.append(_dot(wm[:, hf * gq * CHUNK:(hf + 1) * gq * CHUNK], rhs))
        s_chunks.append(jnp.concatenate(halves, axis=1) + sbias_ref[...])
    s_sgu = jnp.concatenate(s_chunks, axis=0)
    bmix = ugb * s_sgu

    sa = _sigmoid(seg(C_SA, C_SB))
    sb = _sigmoid(seg(C_SB, C_F))
    return qn, kn, v, logf, vbn, ga, bmix, sa, sb


def _in_proj_prompt_kernel(layer, *refs):
    n_prev = 3 if layer else 0
    (x_ref, g_ref, w_ref, bf_ref, qgt_ref, kgt_ref, vg_ref, wcat_ref, sbias_ref, tri_ref, pk_ref,
     pq_ref, qt_ref, ka_ref, vt_ref, ko_ref, vo_ref, lf_ref, ga_ref, bm_ref, sa_ref, sb_ref,
     carry_ref) = refs[n_prev:]
    for prev_ref, out_ref in zip(refs[:n_prev], (ko_ref, vo_ref, lf_ref)):
        out_ref[0:layer] = prev_ref[...]
    sr = vt_ref.shape[-1]

    @pl.when(pl.program_id(1) == 0)
    def _():
        carry_ref[...] = jnp.zeros_like(carry_ref)

    def head_norm_t(zt, gain_t):
        sq = zt * zt
        parts = []
        for hh in range(2):
            ms = jnp.sum(sq[hh * HEAD_DIM:(hh + 1) * HEAD_DIM], axis=0, keepdims=True) * (1.0 / HEAD_DIM)
            parts.append(jnp.broadcast_to(lax.rsqrt(ms + EPS), (HEAD_DIM, sr)))
        return zt * jnp.concatenate(parts, axis=0) * gain_t

    carry = carry_ref[...]
    lane = lax.broadcasted_iota(jnp.int32, (sr, LANES), 1)
    row8 = lax.broadcasted_iota(jnp.int32, (N_HEADS, sr), 0)
    pad_rows = jnp.zeros((LANES - N_HEADS, sr), F32)
    ones_blk = jnp.where(lax.broadcasted_iota(jnp.int32, (VT_ROWS - HEAD_DIM, sr), 0) == 0, 1.0, 0.0)
    for sub in range(x_ref.shape[-2] // sr):
        rows = slice(sub * sr, (sub + 1) * sr)
        zq, zk, v, logf, _, ga, bmix, sa, sb = _in_proj_common(
            x_ref[0, rows], g_ref, w_ref, bf_ref, None, None, vg_ref, None, wcat_ref, sbias_ref, CHUNK)

        lft = logf.T[0:N_HEADS]
        lf_ref[layer, 0, :, rows] = lft
        ga_ref[0, rows] = ga.astype(BF16)
        bm_ref[0, rows] = bmix.astype(BF16)
        sa_ref[0, rows] = sa.astype(BF16)
        sb_ref[0, rows] = sb.astype(BF16)

        stacked = jnp.concatenate([part.astype(F32) for part in _split3(lft)]
                                  + [jnp.zeros((N_HEADS, sr), F32)], axis=0).astype(BF16)
        cs = _dot(stacked, tri_ref[...])
        ct = cs[0:N_HEADS] + cs[N_HEADS:2 * N_HEADS] + cs[2 * N_HEADS:3 * N_HEADS] + carry
        carry = jnp.broadcast_to(ct[:, sr - 1:sr], (N_HEADS, sr))
        c2t = ct * LOG2E
        c2 = jnp.concatenate([c2t, pad_rows], axis=0).T

        hi, mid, lo = _split3(c2)
        cp = (hi.astype(F32) + pltpu.roll(mid.astype(F32), 8, axis=1)
              + pltpu.roll(lo.astype(F32), 16, axis=1) + jnp.where(lane == 24, 1.0, 0.0))
        extra_k = _dot(cp.astype(BF16), pk_ref[...])

        hit, midt, lot = (part.astype(F32) for part in _split3(c2t))
        cpt = jnp.concatenate(
            [hit, midt, lot, jnp.where(row8 == 0, 1.0, 0.0), jnp.zeros((LANES - 32, sr), F32)],
            axis=0).astype(BF16)
        extra_q = _dot(pq_ref[...], cpt)

        for j in range(N_HEADS // 2):
            qs_t = head_norm_t(zq[:, j * LANES:(j + 1) * LANES].T, qgt_ref[...])
            ks_t = head_norm_t(zk[:, j * LANES:(j + 1) * LANES].T, kgt_ref[...])
            vs_t = v[:, j * LANES:(j + 1) * LANES].T
            ks = ks_t.T
            ks_sw = pltpu.roll(ks, HEAD_DIM, axis=1)
            ko_ref[layer, 0, j * LANES:(j + 1) * LANES, rows] = ks_t
            vo_ref[layer, 0, j * LANES:(j + 1) * LANES, rows] = vs_t
            for hh in range(2):
                hd = 2 * j + hh
                qt_ref[0, hd, :, rows] = jnp.concatenate(
                    [qs_t[hh * HEAD_DIM:(hh + 1) * HEAD_DIM], extra_q[hd * HEAD_DIM:(hd + 1) * HEAD_DIM]],
                    axis=0).astype(BF16)
                vt_ref[0, hd, sub] = jnp.concatenate(
                    [vs_t[hh * HEAD_DIM:(hh + 1) * HEAD_DIM], ones_blk], axis=0).astype(BF16)
                ka_ref[0, hd, rows] = jnp.where(lane < HEAD_DIM, ks if hh == 0 else ks_sw,
                                                extra_k).astype(BF16)
    carry_ref[...] = carry


def _in_proj_sample_kernel(x_ref, g_ref, w_ref, bf_ref, qg_ref, kg_ref, vg_ref, bd_ref, wcat_ref,
                           sbias_ref,
                           q_ref, ko_ref, vo_ref, lf_ref, lfp_ref, vbn_ref, ga_ref, bm_ref, sa_ref, sb_ref,
                           *, seq_local):
    qn, kn, v, logf, vbn, ga, bmix, sa, sb = _in_proj_common(
        x_ref[...], g_ref, w_ref, bf_ref, qg_ref, kg_ref, vg_ref, bd_ref, wcat_ref, sbias_ref, seq_local)
    q_ref[...] = qn * (HEAD_DIM ** -0.5 * LOG2E)
    ko_ref[...] = kn
    vo_ref[...] = v
    lf_ref[...] = logf[:, :N_HEADS]
    lfp_ref[...] = logf
    vbn_ref[...] = vbn
    ga_ref[...] = ga.astype(BF16)
    bm_ref[...] = bmix.astype(BF16)
    sa_ref[...] = sa.astype(BF16)
    sb_ref[...] = sb.astype(BF16)


def _const_spec(arr):
    if isinstance(arr, tuple):
        return tuple(_const_spec(a) for a in arr)
    return pl.BlockSpec(arr.shape, lambda *_: (0,) * arr.ndim, pipeline_mode=pl.Buffered(1))


def _in_proj_consts(tm):
    bd = np.kron(np.eye(N_HEADS, dtype=np.float32), np.ones((HEAD_DIM, HEAD_DIM), np.float32))
    tri = np.triu(np.ones((tm, tm), np.float32))
    pk = np.zeros((LANES, LANES), np.float32)
    pq = np.zeros((N_HEADS * HEAD_DIM, LANES), np.float32)
    for hd in range(N_HEADS):
        for part in range(3):
            pk[8 * part + hd, HEAD_DIM + 3 * hd + part] = -1.0
            pk[24, HEAD_DIM + 3 * N_HEADS + part] = 1.0
            pq[hd * HEAD_DIM + 3 * hd + part, 24] = 1.0
            pq[hd * HEAD_DIM + 3 * N_HEADS + part, 8 * part + hd] = 1.0
    return (jnp.asarray(bd, BF16), jnp.asarray(tri, BF16), jnp.asarray(pk, BF16), jnp.asarray(pq, BF16))


def _in_proj_prompt(x, params, tm, slab, layer, stacked):
    b, s, _ = x.shape
    nt = s // tm
    _, tri, pk, pq = _in_proj_consts(slab)
    g, w, bf, qgt, kgt, vg, wcat, sbias = params
    consts = (g, w, bf, qgt, kgt, vg, wcat, sbias, tri, pk, pq)
    stacked = () if stacked is None else tuple(stacked)
    assert len(stacked) == (3 if layer else 0)
    row3 = lambda width: pl.BlockSpec((1, tm, width), lambda bi, i: (bi, i, 0))
    col4 = lambda n, rows: pl.BlockSpec((n, 1, rows, tm), lambda bi, i: (0, bi, 0, i))
    depth = layer + 1
    out_shape = (
        jax.ShapeDtypeStruct((b, N_HEADS, LANES, s), BF16),
        jax.ShapeDtypeStruct((b, N_HEADS, s, LANES), BF16),
        jax.ShapeDtypeStruct((b, N_HEADS, s // slab, VT_ROWS, slab), BF16),
        jax.ShapeDtypeStruct((depth, b, WIDTH_A, s), F32),
        jax.ShapeDtypeStruct((depth, b, WIDTH_A, s), F32),
        jax.ShapeDtypeStruct((depth, b, N_HEADS, s), F32),
        jax.ShapeDtypeStruct((b, s, WIDTH_A), BF16),
        jax.ShapeDtypeStruct((b, s, WIDTH_B), BF16),
        jax.ShapeDtypeStruct((b, s, D_MODEL), BF16),
        jax.ShapeDtypeStruct((b, s, D_MODEL), BF16),
    )
    out_specs = (
        pl.BlockSpec((1, N_HEADS, LANES, tm), lambda bi, i: (bi, 0, 0, i)),
        pl.BlockSpec((1, N_HEADS, tm, LANES), lambda bi, i: (bi, 0, i, 0)),
        pl.BlockSpec((1, N_HEADS, tm // slab, VT_ROWS, slab), lambda bi, i: (bi, 0, i, 0, 0)),
        col4(depth, WIDTH_A), col4(depth, WIDTH_A), col4(depth, N_HEADS), row3(WIDTH_A), row3(WIDTH_B),
        row3(D_MODEL), row3(D_MODEL),
    )
    prev_specs = [col4(layer, WIDTH_A), col4(layer, WIDTH_A), col4(layer, N_HEADS)] if layer else []
    return pl.pallas_call(
        functools.partial(_in_proj_prompt_kernel, layer),
        out_shape=out_shape,
        grid=(b, nt),
        in_specs=prev_specs + [row3(D_MODEL)] + [_const_spec(c) for c in consts],
        out_specs=out_specs,
        scratch_shapes=[pltpu.VMEM((N_HEADS, slab), F32)],
        compiler_params=pltpu.CompilerParams(
            dimension_semantics=("arbitrary", "arbitrary"), vmem_limit_bytes=VMEM_LIMIT),
        name="in_proj_prompt",
    )(*stacked, x, *consts)


def _in_proj_sample(x, params, tm, seq_local):
    n = x.shape[0]
    bd = _in_proj_consts(tm)[0]
    g, w, bf, qg, kg, vg, wcat, sbias = params
    consts = (g, w, bf, qg, kg, vg, bd, wcat, sbias)
    row2 = lambda width: pl.BlockSpec((tm, width), lambda i: (i, 0))
    out_shape = (
        jax.ShapeDtypeStruct((n, WIDTH_A), F32),
        jax.ShapeDtypeStruct((n, WIDTH_A), F32),
        jax.ShapeDtypeStruct((n, WIDTH_A), F32),
        jax.ShapeDtypeStruct((n, N_HEADS), F32),
        jax.ShapeDtypeStruct((n, LANES), F32),
        jax.ShapeDtypeStruct((n, WIDTH_B), F32),
        jax.ShapeDtypeStruct((n, WIDTH_A), BF16),
        jax.ShapeDtypeStruct((n, WIDTH_B), BF16),
        jax.ShapeDtypeStruct((n, D_MODEL), BF16),
        jax.ShapeDtypeStruct((n, D_MODEL), BF16),
    )
    out_specs = (row2(WIDTH_A), row2(WIDTH_A), row2(WIDTH_A), row2(N_HEADS), row2(LANES), row2(WIDTH_B),
                 row2(WIDTH_A), row2(WIDTH_B), row2(D_MODEL), row2(D_MODEL))
    return pl.pallas_call(
        functools.partial(_in_proj_sample_kernel, seq_local=seq_local),
        out_shape=out_shape,
        grid=(n // tm,),
        in_specs=[row2(D_MODEL)] + [_const_spec(c) for c in consts],
        out_specs=out_specs,
        compiler_params=pltpu.CompilerParams(
            dimension_semantics=("arbitrary",), vmem_limit_bytes=VMEM_LIMIT),
        name="in_proj_sample",
    )(x, *consts)


def _attn_prompt_kernel(qt_ref, ka_ref, vt_ref, o_ref, s_ref, m_ref, al_ref, acc_ref):
    g = qt_ref.shape[1]
    tq = qt_ref.shape[-1]
    t = vt_ref.shape[-1]
    assert tq == 2 * t
    i = pl.program_id(2)
    key = lax.broadcasted_iota(jnp.int32, (t, tq), 0)
    qry = lax.broadcasted_iota(jnp.int32, (t, tq), 1)

    def scores(hd, blk, slot, mask, m_old):
        k = ka_ref[0, hd, pl.ds(pl.multiple_of(blk * t, t), t), :]
        s = _dot(k, qt_ref[0, hd])
        if mask is not None:
            s = jnp.where(mask, s, NEG_BIG)
        s_ref[slot, hd] = s
        m_new = jnp.maximum(m_old, jnp.max(s, axis=0, keepdims=True))
        m_ref[slot, hd, 0:1, :] = m_new
        al_ref[slot, hd, 0:1, :] = jnp.exp2(m_old - m_new)

    def iteration(j, cur, next_mask, has_next):
        for hd in range(g):
            p = jnp.exp2(s_ref[cur, hd] - m_ref[cur, hd, 0:1, :]).astype(BF16)
            if has_next:
                scores(hd, j + 1, 1 - cur, next_mask, m_ref[cur, hd, 0:1, :])
            acc_ref[hd] = al_ref[cur, hd, 0:1, :] * acc_ref[hd] + _dot(vt_ref[0, hd, j], p)

    acc_ref[...] = jnp.zeros_like(acc_ref)
    diag_lo = key <= qry
    diag_hi = key + t <= qry
    for hd in range(g):
        scores(hd, 0, 0, key <= qry + jnp.where(i > 0, tq, 0), jnp.full((1, tq), NEG_BIG, F32))

    def pair(jj, carry):
        iteration(2 * jj, 0, None, True)
        iteration(2 * jj + 1, 1, None, True)
        return carry

    lax.fori_loop(0, i - 1, pair, 0)

    @pl.when(i >= 1)
    def _():
        iteration(2 * i - 2, 0, None, True)
        iteration(2 * i - 1, 1, diag_lo, True)

    last = 2 * i + 1
    iteration(last - 1, 0, diag_hi, True)
    iteration(last, 1, None, False)
    for hd in range(g):
        acc = acc_ref[hd]
        o_ref[0, hd * HEAD_DIM:(hd + 1) * HEAD_DIM, :] = acc[0:HEAD_DIM] / acc[HEAD_DIM:HEAD_DIM + 1]


def _attn_prompt(qt, ka, vt, g):
    b, h, _, s = qt.shape
    t = vt.shape[-1]
    tq = 2 * t
    return pl.pallas_call(
        _attn_prompt_kernel,
        out_shape=jax.ShapeDtypeStruct((b, WIDTH_A, s), F32),
        grid=(b, h // g, s // tq),
        in_specs=[
            pl.BlockSpec((1, g, LANES, tq), lambda bi, hi, i: (bi, hi, 0, i)),
            pl.BlockSpec((1, g, s, LANES), lambda bi, hi, i: (bi, hi, 0, 0),
                         pipeline_mode=pl.Buffered(1)),
            pl.BlockSpec((1, g, s // t, VT_ROWS, t), lambda bi, hi, i: (bi, hi, 0, 0, 0),
                         pipeline_mode=pl.Buffered(1)),
        ],
        out_specs=pl.BlockSpec((1, g * HEAD_DIM, tq), lambda bi, hi, i: (bi, hi, i)),
        scratch_shapes=[pltpu.VMEM((2, g, t, tq), F32),
                        pltpu.VMEM((2, g, 8, tq), F32), pltpu.VMEM((2, g, 8, tq), F32),
                        pltpu.VMEM((g, VT_ROWS, tq), F32)],
        compiler_params=pltpu.CompilerParams(
            dimension_semantics=("arbitrary", "arbitrary", "arbitrary"), vmem_limit_bytes=VMEM_LIMIT),
        name="attn_prompt",
    )(qt, ka, vt)


def _attn_sample_one(q, k_new, v_new, logf_new, ls_ref, ones_ref, m3_ref, k_refs, v_refs, l_refs):
    t_new = q.shape[0]
    n_pages = len(k_refs)
    rows = t_new * N_HEADS

    def per_token(x):
        return jnp.concatenate(
            [jnp.broadcast_to(x[tk:tk + 1], (N_HEADS, x.shape[1])) for tk in range(t_new)], axis=0)

    row = lax.broadcasted_iota(jnp.int32, (rows, WIDTH_A), 0)
    lane = lax.broadcasted_iota(jnp.int32, (rows, WIDTH_A), 1)
    own_head = (row & (N_HEADS - 1)) == (lane >> HEAD_DIM_BITS)
    qbd = jnp.where(own_head, per_token(q), 0.0).astype(BF16)

    cn = logf_new * LOG2E
    trow = lax.broadcasted_iota(jnp.int32, (t_new, LANES), 0)
    sh = 1
    while sh < t_new:
        cn = cn + jnp.where(trow >= sh, pltpu.roll(cn, sh, axis=0), 0.0)
        sh *= 2
    row1 = lax.broadcasted_iota(jnp.int32, (rows, LANES), 0)
    lane1 = lax.broadcasted_iota(jnp.int32, (rows, LANES), 1)
    head_lane = (row1 & (N_HEADS - 1)) == lane1
    cn_col = jnp.sum(jnp.where(head_lane, per_token(cn), 0.0), axis=1, keepdims=True)

    x = jnp.concatenate([l_refs[p][...] for p in range(n_pages)], axis=0) * LOG2E
    xp = _split3(x)
    r_in = _dot3_right(xp, ls_ref[...])
    tot_b = _dot3_right(xp, ones_ref[...])
    r_page = _dot3_left(m3_ref[...], _split3(tot_b))
    r = r_in + r_page

    s_blocks = []
    for p in range(n_pages):
        kp_t = k_refs[p][...].astype(BF16)
        r_p = r[p * N_HEADS:(p + 1) * N_HEADS]
        s_blocks.append(_dot(qbd, kp_t) + jnp.concatenate([r_p] * t_new, axis=0) + cn_col)
    pad = jnp.zeros((PAGE - t_new, WIDTH_A), F32)
    kn = jnp.concatenate([k_new, pad], axis=0).astype(BF16)
    vn = jnp.concatenate([v_new, pad], axis=0).astype(BF16)
    cn_pad = jnp.concatenate([cn, jnp.zeros((PAGE - t_new, LANES), F32)], axis=0)
    sel = jnp.where(head_lane, 1.0, 0.0).astype(BF16)
    cn_keys = sum(_dot_nt(sel, part) for part in _split3(cn_pad))
    s_new = _dot_nt(qbd, kn) + cn_col - cn_keys
    s_new = jnp.where((lane1 <= (row1 >> HEAD_BITS)) & (lane1 < t_new), s_new, NEG_BIG)
    s_blocks.append(s_new)

    m = s_blocks[0]
    for sb in s_blocks[1:]:
        m = jnp.maximum(m, sb)
    m = jnp.max(m, axis=1, keepdims=True)
    l_sum = jnp.zeros((rows, 1), F32)
    o = jnp.zeros((rows, WIDTH_A), F32)
    for p in range(n_pages + 1):
        pe = jnp.exp2(s_blocks[p] - m)
        l_sum = l_sum + jnp.sum(pe, axis=1, keepdims=True)
        if p < n_pages:
            o = o + _dot_nt(pe.astype(BF16), v_refs[p][...].astype(BF16))
        else:
            o = o + _dot(pe.astype(BF16), vn)
    o = jnp.where(own_head, o / l_sum, 0.0)
    return jnp.concatenate(
        [jnp.sum(o[tk * N_HEADS:(tk + 1) * N_HEADS], axis=0, keepdims=True) for tk in range(t_new)], axis=0)


def _attn_sample_consts(n_pages):
    key = np.arange(PAGE)
    ls = (key[:, None] > key[None, :]).astype(np.float32)
    ones = np.ones((PAGE, LANES), np.float32)
    ph = np.arange(n_pages * N_HEADS)
    m3 = ((ph[:, None] % N_HEADS == ph[None, :] % N_HEADS)
          & (ph[None, :] // N_HEADS > ph[:, None] // N_HEADS)).astype(np.float32)
    return tuple(jnp.asarray(a, BF16) for a in (ls, ones, m3))


def _merge_kernel(transposed, slab, a_ref, ga_ref, bm_ref, sa_ref, sb_ref, x_ref, wpa_ref, wpb_ref, wo_ref,
                  y_ref):
    for sub in range(x_ref.shape[-2] // slab):
        rows = slice(sub * slab, (sub + 1) * slab)
        at = (lambda ref: ref[0, rows]) if transposed else (lambda ref: ref[rows])
        attn = a_ref[0, :, rows].T if transposed else a_ref[rows]
        a = (attn * at(ga_ref).astype(F32)).astype(BF16)
        pa = _dot(a, wpa_ref[...])
        pb = _dot(at(bm_ref), wpb_ref[...])
        m = at(sa_ref).astype(F32) * pa + at(sb_ref).astype(F32) * pb
        y = at(x_ref) + _dot(m.astype(BF16), wo_ref[...])
        if transposed:
            y_ref[0, rows] = y
        else:
            y_ref[rows] = y


def _merge_attn_kernel(layer, n_pages, t_new, seqs, pt_ref, a_ref, ga_ref, bm_ref, sa_ref, sb_ref, x_ref,
                       wpa_ref, wpb_ref, wo_ref, q_ref, kn_ref, vn_ref, ln_ref, ls_ref, ones_ref, m3_ref,
                       ck_hbm, cv_hbm, cl_hbm, y_ref, o_ref, kbuf, vbuf, lbuf, sems):
    nt = pl.num_programs(1)
    st = pl.program_id(0) * nt + pl.program_id(1)
    last = pl.num_programs(0) * nt - 1
    slot = st & 1
    per_step = seqs * n_pages
    kinds = ((ck_hbm, kbuf), (cv_hbm, vbuf), (cl_hbm, lbuf))

    def page_copy(kind, page, to_slot, idx):
        hbm, buf = kinds[kind]
        return pltpu.make_async_copy(hbm.at[layer, page], buf.at[to_slot, idx], sems.at[to_slot, kind])

    def fetch(step, to_slot):
        for idx in range(per_step):
            page = pt_ref[step * seqs + idx // n_pages, idx % n_pages]
            for kind in range(3):
                page_copy(kind, page, to_slot, idx).start()

    @pl.when(st == 0)
    def _():
        fetch(0, 0)

    @pl.when(st < last)
    def _():
        fetch(st + 1, 1 - slot)

    _merge_kernel(True, x_ref.shape[-2], a_ref, ga_ref, bm_ref, sa_ref, sb_ref, x_ref, wpa_ref, wpb_ref,
                  wo_ref, y_ref)

    for idx in range(per_step):
        for kind in range(3):
            page_copy(kind, 0, slot, idx).wait()
    for u in range(seqs):
        tok = slice(u * t_new, (u + 1) * t_new)
        pages = [[buf.at[slot, u * n_pages + p] for p in range(n_pages)] for _, buf in kinds]
        o_ref[tok] = _attn_sample_one(q_ref[tok], kn_ref[tok], vn_ref[tok], ln_ref[tok],
                                      ls_ref, ones_ref, m3_ref, *pages)


def _merge_prompt_attn_sample(layer, page_table, attn_t, ga, bm, sa, sb, x, wpa, wpb, wo, q, kn, vn, lnp,
                              ck, cv, clt, t_new, seqs):
    b, s, _ = x.shape
    n_seq, n_pages = page_table.shape
    nt = n_seq // (seqs * b)
    tm = s // nt
    assert b * nt * seqs == n_seq and nt * tm == s
    consts = _attn_sample_consts(n_pages)
    step = lambda bi, i: bi * nt + i
    row3 = lambda width: pl.BlockSpec((1, tm, width), lambda bi, i, pt: (bi, i, 0))
    tok = lambda width: pl.BlockSpec((seqs * t_new, width), lambda bi, i, pt: (step(bi, i), 0))

    per_step = seqs * n_pages
    grid_spec = pltpu.PrefetchScalarGridSpec(
        num_scalar_prefetch=1,
        grid=(b, nt),
        in_specs=[pl.BlockSpec((1, WIDTH_A, tm), lambda bi, i, pt: (bi, 0, i)),
                  row3(WIDTH_A), row3(WIDTH_B), row3(D_MODEL), row3(D_MODEL), row3(D_MODEL),
                  _const_spec(wpa), _const_spec(wpb), _const_spec(wo),
                  tok(WIDTH_A), tok(WIDTH_A), tok(WIDTH_A), tok(LANES)]
        + [_const_spec(c) for c in consts]
        + [pl.BlockSpec(memory_space=pl.ANY)] * 3,
        out_specs=(row3(D_MODEL), tok(WIDTH_A)),
        scratch_shapes=[pltpu.VMEM((2, per_step, WIDTH_A, PAGE), F32),
                        pltpu.VMEM((2, per_step, WIDTH_A, PAGE), F32),
                        pltpu.VMEM((2, per_step, N_HEADS, PAGE), F32),
                        pltpu.SemaphoreType.DMA((2, 3))],
    )
    return pl.pallas_call(
        functools.partial(_merge_attn_kernel, layer, n_pages, t_new, seqs),
        out_shape=(jax.ShapeDtypeStruct((b, s, D_MODEL), F32),
                   jax.ShapeDtypeStruct((n_seq * t_new, WIDTH_A), F32)),
        grid_spec=grid_spec,
        compiler_params=pltpu.CompilerParams(
            dimension_semantics=("arbitrary", "arbitrary"), vmem_limit_bytes=VMEM_LIMIT),
        name="merge_prompt_attn_sample",
    )(page_table, attn_t, ga, bm, sa, sb, x, wpa, wpb, wo, q, kn, vn, lnp, *consts, ck, cv, clt)


def _merge_sample(attn, ga, bm, sa, sb, x, wpa, wpb, wo, tm):
    n = x.shape[0]
    row2 = lambda width: pl.BlockSpec((tm, width), lambda i: (i, 0))
    return pl.pallas_call(
        functools.partial(_merge_kernel, False, tm),
        out_shape=jax.ShapeDtypeStruct((n, D_MODEL), F32),
        grid=(n // tm,),
        in_specs=[row2(WIDTH_A), row2(WIDTH_A), row2(WIDTH_B), row2(D_MODEL), row2(D_MODEL), row2(D_MODEL),
                  _const_spec(wpa), _const_spec(wpb), _const_spec(wo)],
        out_specs=row2(D_MODEL),
        compiler_params=pltpu.CompilerParams(
            dimension_semantics=("arbitrary",), vmem_limit_bytes=VMEM_LIMIT),
        name="merge_sample",
    )(attn, ga, bm, sa, sb, x, wpa, wpb, wo)


KEY_BLOCK = 256
TM_PROMPT = 512
HEADS_PER_STEP = 8
SEQS_PER_STEP = 2
TM_SAMPLE = 256


def kernel(x_prompt, x_sample, cache_k, cache_v, cache_logf, page_table, norm_g, w_in, b_forget, q_norm_g,
           k_norm_g, v_norm_g, w_s, b_s, w_proj_a, w_proj_b, w_out):
    depth = w_in.shape[0]
    bsz, seq, _ = x_prompt.shape
    n_seq, t_new, _ = x_sample.shape
    n_pool = cache_k.shape[1]

    f_lo = 4 * WIDTH_A
    w_head = w_in[:, :, :f_lo].astype(BF16)
    w_tail = w_in[:, :, f_lo + N_HEADS:].astype(BF16)
    w_forget = jnp.pad(w_in[:, :, f_lo:f_lo + N_HEADS],
                       ((0, 0), (0, 0), (0, C_END - C_F - N_HEADS))).astype(BF16)
    bf_pad = jnp.pad(b_forget, ((0, 0), (0, LANES - N_HEADS)))[:, None, :]
    qg = jnp.tile(q_norm_g, (1, N_HEADS))[:, None, :]
    kg = jnp.tile(k_norm_g, (1, N_HEADS))[:, None, :]
    gain_cols = lambda gn: jnp.tile(gn[:, :, None], (1, 2, KEY_BLOCK))
    qg_t = gain_cols(q_norm_g * (HEAD_DIM ** -0.5 * LOG2E))
    kg_t = gain_cols(k_norm_g)
    vg = v_norm_g[:, None, :]
    ng = norm_g[:, None, :]
    wcat_p = w_s.transpose(0, 2, 1, 3).reshape(depth, CHUNK, N_GROUPS * CHUNK).astype(BF16)
    w_small = jnp.tile(w_s[:, :, :t_new, :t_new], (1, 1, CHUNK // t_new, CHUNK // t_new))
    wcat_s = w_small.transpose(0, 2, 1, 3).reshape(depth, CHUNK, N_GROUPS * CHUNK).astype(BF16)
    sbias_p = jnp.repeat(b_s.transpose(0, 2, 1), GROUP_DIM, axis=-1)
    sbias_s = jnp.repeat(jnp.tile(b_s[:, :, :t_new], (1, 1, CHUNK // t_new)).transpose(0, 2, 1),
                         GROUP_DIM, axis=-1)
    wpa = w_proj_a.astype(BF16)
    wpb = w_proj_b.astype(BF16)
    wo = w_out.astype(BF16)
    ck = cache_k.transpose(0, 1, 3, 4, 2).reshape(depth, n_pool, WIDTH_A, PAGE)
    cv = cache_v.transpose(0, 1, 3, 4, 2).reshape(depth, n_pool, WIDTH_A, PAGE)
    clt = cache_logf.transpose(0, 1, 3, 2)

    xp = x_prompt
    xs = x_sample.reshape(n_seq * t_new, D_MODEL)
    outs = [[] for _ in range(4)]
    stacked = None
    for l in range(depth):
        weights = (w_head[l], w_tail[l], w_forget[l])
        qt, ka, vt, *stacked, ga, bm, sa, sb = _in_proj_prompt(
            xp, (ng[l], weights, bf_pad[l], qg_t[l], kg_t[l], vg[l], wcat_p[l], sbias_p[l]),
            TM_PROMPT, KEY_BLOCK, l, stacked)
        common = (ng[l], weights, bf_pad[l], qg[l], kg[l], vg[l])
        attn_t = _attn_prompt(qt, ka, vt, HEADS_PER_STEP)

        q_s, k_s, v_s, lf_s, lfp_s, vbn_s, ga_s, bm_s, sa_s, sb_s = _in_proj_sample(
            xs, common + (wcat_s[l], sbias_s[l]), TM_SAMPLE, t_new)
        xp, attn_s = _merge_prompt_attn_sample(
            l, page_table, attn_t, ga, bm, sa, sb, xp, wpa[l], wpb[l], wo[l], q_s, k_s, v_s, lfp_s,
            ck, cv, clt, t_new, SEQS_PER_STEP)
        xs = _merge_sample(attn_s, ga_s, bm_s, sa_s, sb_s, xs, wpa[l], wpb[l], wo[l], TM_SAMPLE)
        outs[0].append(k_s.reshape(n_seq, t_new, N_HEADS, HEAD_DIM))
        outs[1].append(v_s.reshape(n_seq, t_new, N_HEADS, HEAD_DIM))
        outs[2].append(lf_s.reshape(n_seq, t_new, N_HEADS))
        outs[3].append(vbn_s.reshape(n_seq, t_new, WIDTH_B))
    kt_all, vt_all, lft_all = stacked
    unstack = lambda a: a.reshape(depth, bsz, N_HEADS, HEAD_DIM, seq).transpose(0, 1, 4, 2, 3)
    return ((xp, xs.reshape(n_seq, t_new, D_MODEL), unstack(kt_all), unstack(vt_all),
             lft_all.transpose(0, 1, 3, 2)) + tuple(jnp.stack(o) for o in outs))
```
